```python
import jax, jax.numpy as jnp
from jax import lax
import numpy as np

D_MODEL = 2048
BATCH = 4
SEQ = 4096
DEPTH = 4

GLA_HEADS = 4
GLA_DK = D_MODEL // 2 // GLA_HEADS
GLA_DV = D_MODEL // GLA_HEADS
GLA_RANK = 16
GLA_TAU = 16.0
GLA_CHUNK = 64
CONV_DIM = D_MODEL
CONV_WIDTH = 31
N_BRANCH = 2
FFN_DIM = 5632
N_EXPERTS = 8
TOP_K = 2
MOE_BLOCK = 512
N_DENSE = (DEPTH + 1) // 2
N_MOE = DEPTH // 2
DN_ALPHA = (2 * DEPTH) ** 0.25
DN_BETA = (8 * DEPTH) ** -0.25
LN_EPS = 1e-5
IN_SIZES = (GLA_HEADS * GLA_DK, GLA_HEADS * GLA_DK, GLA_HEADS * GLA_DV, GLA_HEADS * GLA_DV,
            GLA_RANK, 2 * CONV_DIM, N_BRANCH * D_MODEL)
IN_DIM = sum(IN_SIZES)

kernel_name = "gla_conformer_gated_hybrid_deepnorm_moe"


def layer_norm(x, g, b):
    xf = x.astype(jnp.float32)
    mu = xf.mean(-1, keepdims=True)
    var = jnp.square(xf - mu).mean(-1, keepdims=True)
    return ((xf - mu) * lax.rsqrt(var + LN_EPS) * g + b).astype(x.dtype)


def gla_mixer(q, k, v, r, a_low, w_a_up, b_a, g_norm, w_o):
    B, T, _ = q.shape
    C = GLA_CHUNK
    N = T // C
    f32 = jnp.float32
    log_a = jax.nn.log_sigmoid((a_low @ w_a_up + b_a).astype(f32)) / GLA_TAU

    def heads(t, d):
        return t.reshape(B, N, C, GLA_HEADS, d).transpose(0, 3, 1, 2, 4)

    qh = heads(q, GLA_DK).astype(f32) * (GLA_DK ** -0.5)
    kh = heads(k, GLA_DK).astype(f32)
    vh = heads(v, GLA_DV).astype(f32)
    cum = jnp.cumsum(heads(log_a, GLA_DK), axis=3)
    last = cum[..., -1:, :]
    q_in = qh * jnp.exp(cum)
    k_in = kh * jnp.exp(-cum)
    k_st = kh * jnp.exp(last - cum)
    decay = jnp.exp(last[..., 0, :])
    causal = jnp.tril(jnp.ones((C, C), dtype=bool))
    att = jnp.where(causal, jnp.einsum('bhnid,bhnjd->bhnij', q_in, k_in), 0.0)
    o_intra = jnp.einsum('bhnij,bhnjv->bhniv', att, vh)

    def step(S, xs):
        q_c, k_c, v_c, d_c = xs
        o = jnp.einsum('bhcd,bhdv->bhcv', q_c, S)
        S = d_c[..., None] * S + jnp.einsum('bhcd,bhcv->bhdv', k_c, v_c)
        return S, o

    S0 = jnp.zeros((B, GLA_HEADS, GLA_DK, GLA_DV), f32)
    cm = lambda t: jnp.moveaxis(t, 2, 0)
    _, o_inter = lax.scan(step, S0, (cm(q_in), cm(k_st), cm(vh), cm(decay)))
    o = o_intra + jnp.moveaxis(o_inter, 0, 2)
    o = o * lax.rsqrt(jnp.mean(jnp.square(o), -1, keepdims=True) + LN_EPS) * g_norm
    o = o.transpose(0, 2, 3, 1, 4).reshape(B, T, GLA_HEADS * GLA_DV).astype(v.dtype)
    return (o * jax.nn.silu(r)) @ w_o


def conv_mixer(u, w_dw, b_dw, ln_g, ln_b, w_pw):
    val, gate = jnp.split(u, 2, axis=-1)
    h = val * jax.nn.sigmoid(gate)
    h = lax.conv_general_dilated(
        h, w_dw[:, None, :].astype(h.dtype), window_strides=(1,),
        padding=[(CONV_WIDTH - 1, 0)], dimension_numbers=('NWC', 'WIO', 'NWC'),
        feature_group_count=CONV_DIM) + b_dw
    h = jax.nn.silu(layer_norm(h, ln_g, ln_b))
    return h @ w_pw


def swiglu(h, w_gate, w_up, w_down):
    return (jax.nn.silu(h @ w_gate) * (h @ w_up)) @ w_down


def moe_swiglu(h, w_router, w_gate, w_up, w_down):
    B, T, D = h.shape
    tokens = h.reshape(-1, D)
    n = B * T
    n_assign = n * TOP_K
    logits = (tokens @ w_router).astype(jnp.float32)
    top_logit, top_idx = lax.top_k(logits, TOP_K)
    top_w = jax.nn.softmax(top_logit, axis=-1)
    flat_e = top_idx.reshape(-1)
    flat_w = top_w.reshape(-1)
    flat_tok = jnp.arange(n_assign, dtype=jnp.int32) // TOP_K
    order = jnp.argsort(flat_e)
    e_sorted = flat_e[order]
    counts = jnp.bincount(flat_e, length=N_EXPERTS)
    padded = (counts + MOE_BLOCK - 1) // MOE_BLOCK * MOE_BLOCK
    start = jnp.cumsum(counts) - counts
    pend = jnp.cumsum(padded)
    pstart = pend - padded
    dest = pstart[e_sorted] + (jnp.arange(n_assign, dtype=jnp.int32) - start[e_sorted])
    n_rows = (-(-n_assign // MOE_BLOCK) + N_EXPERTS) * MOE_BLOCK
    n_blocks = n_rows // MOE_BLOCK
    row_tok = jnp.zeros((n_rows,), jnp.int32).at[dest].set(flat_tok[order])
    row_w = jnp.zeros((n_rows,), jnp.float32).at[dest].set(flat_w[order])
    block_e = jnp.minimum(
        jnp.searchsorted(pend, jnp.arange(n_blocks, dtype=jnp.int32) * MOE_BLOCK, side='right'),
        N_EXPERTS - 1)
    xb = tokens[row_tok].reshape(n_blocks, MOE_BLOCK, D)

    def expert_block(args):
        xe, e = args
        return swiglu(xe, w_gate[e], w_up[e], w_down[e])

    yb = lax.map(expert_block, (xb, block_e)).reshape(n_rows, D)
    y = jax.ops.segment_sum(yb * row_w[:, None].astype(yb.dtype), row_tok, num_segments=n)
    return y.reshape(B, T, D).astype(h.dtype)


def setup_inputs(seed: int = 0) -> dict:
    key = jax.random.key(seed)
    ks = jax.random.split(key, 24)
    f32 = jnp.float32
    nrm = lambda k, shape, s: jax.random.normal(k, shape, f32) * s
    return {
        "x": nrm(ks[0], (BATCH, SEQ, D_MODEL), 1.0),
        "w_in": nrm(ks[1], (DEPTH, D_MODEL, IN_DIM), D_MODEL ** -0.5),
        "gla_a_up": nrm(ks[2], (DEPTH, GLA_RANK, GLA_HEADS * GLA_DK), GLA_RANK ** -0.5),
        "gla_a_bias": nrm(ks[3], (DEPTH, GLA_HEADS * GLA_DK), 0.02),
        "gla_norm_g": 1.0 + nrm(ks[4], (DEPTH, GLA_DV), 0.02),
        "gla_w_o": nrm(ks[5], (DEPTH, GLA_HEADS * GLA_DV, D_MODEL), (GLA_HEADS * GLA_DV) ** -0.5),
        "conv_w_dw": nrm(ks[6], (DEPTH, CONV_WIDTH, CONV_DIM), CONV_WIDTH ** -0.5),
        "conv_b_dw": nrm(ks[7], (DEPTH, CONV_DIM), 0.02),
        "conv_ln_g": 1.0 + nrm(ks[8], (DEPTH, CONV_DIM), 0.02),
        "conv_ln_b": nrm(ks[9], (DEPTH, CONV_DIM), 0.02),
        "conv_w_pw": nrm(ks[10], (DEPTH, CONV_DIM, D_MODEL), CONV_DIM ** -0.5),
        "w_out": nrm(ks[11], (DEPTH, D_MODEL, D_MODEL), DN_BETA * D_MODEL ** -0.5),
        "ln_g": 1.0 + nrm(ks[12], (DEPTH, 2, D_MODEL), 0.02),
        "ln_b": nrm(ks[13], (DEPTH, 2, D_MODEL), 0.02),
        "ffn_w_gate": nrm(ks[14], (N_DENSE, D_MODEL, FFN_DIM), D_MODEL ** -0.5),
        "ffn_w_up": nrm(ks[15], (N_DENSE, D_MODEL, FFN_DIM), D_MODEL ** -0.5),
        "ffn_w_down": nrm(ks[16], (N_DENSE, FFN_DIM, D_MODEL), DN_BETA * FFN_DIM ** -0.5),
        "moe_w_router": nrm(ks[17], (N_MOE, D_MODEL, N_EXPERTS), D_MODEL ** -0.5),
        "moe_w_gate": nrm(ks[18], (N_MOE, N_EXPERTS, D_MODEL, FFN_DIM), D_MODEL ** -0.5),
        "moe_w_up": nrm(ks[19], (N_MOE, N_EXPERTS, D_MODEL, FFN_DIM), D_MODEL ** -0.5),
        "moe_w_down": nrm(ks[20], (N_MOE, N_EXPERTS, FFN_DIM, D_MODEL), DN_BETA * FFN_DIM ** -0.5),
    }


def reference(x, w_in, gla_a_up, gla_a_bias, gla_norm_g, gla_w_o, conv_w_dw, conv_b_dw,
              conv_ln_g, conv_ln_b, conv_w_pw, w_out, ln_g, ln_b, ffn_w_gate, ffn_w_up,
              ffn_w_down, moe_w_router, moe_w_gate, moe_w_up, moe_w_down):
    split_points = np.cumsum(IN_SIZES)[:-1].tolist()
    for layer in range(DEPTH):
        u = x @ w_in[layer]
        q, k, v, r, a_low, conv_u, gate_logits = jnp.split(u, split_points, axis=-1)
        y_a = gla_mixer(q, k, v, r, a_low, gla_a_up[layer], gla_a_bias[layer],
                        gla_norm_g[layer], gla_w_o[layer])
        y_b = conv_mixer(conv_u, conv_w_dw[layer], conv_b_dw[layer], conv_ln_g[layer],
                         conv_ln_b[layer], conv_w_pw[layer])
        g_a, g_b = jnp.split(jax.nn.sigmoid(gate_logits), 2, axis=-1)
        mix = (g_a * y_a + g_b * y_b) @ w_out[layer]
        x = layer_norm(DN_ALPHA * x + mix, ln_g[layer, 0], ln_b[layer, 0])
        if layer % 2 == 0:
            i = layer // 2
            f = swiglu(x, ffn_w_gate[i], ffn_w_up[i], ffn_w_down[i])
        else:
            i = layer // 2
            f = moe_swiglu(x, moe_w_router[i], moe_w_gate[i], moe_w_up[i], moe_w_down[i])
        x = layer_norm(DN_ALPHA * x + f, ln_g[layer, 1], ln_b[layer, 1])
    return x
```

```python
import functools

import jax
import jax.numpy as jnp
from jax import lax
from jax.experimental import pallas as pl
from jax.experimental.pallas import tpu as pltpu

D_MODEL = 2048
DEPTH = 4
GLA_HEADS = 4
GLA_DK = 256
GLA_DV = 512
GLA_RANK = 16
GLA_TAU = 16.0
GLA_CHUNK = 64
CONV_WIDTH = 31
FFN_DIM = 5632
N_EXPERTS = 8
TOP_K = 2
DN_ALPHA = (2 * DEPTH) ** 0.25
LN_EPS = 1e-5

LANES = 128
QKVR = 2 * GLA_HEADS * GLA_DK + 2 * GLA_HEADS * GLA_DV
U_MAIN = QKVR + 4 * D_MODEL
VMEM_LIMIT = 56 * 1024 * 1024

F32 = jnp.float32
BF16 = jnp.bfloat16


def _cparams(sem):
    return pltpu.CompilerParams(dimension_semantics=sem, vmem_limit_bytes=VMEM_LIMIT)


def _dot(a, b):
    return jnp.dot(a, b, preferred_element_type=F32)


def _sigmoid(x):
    return 1.0 / (1.0 + jnp.exp(-x))


def _layer_norm(y, g, b):
    mu = jnp.mean(y, axis=-1, keepdims=True)
    d = y - mu
    var = jnp.mean(d * d, axis=-1, keepdims=True)
    return d * lax.rsqrt(var + LN_EPS) * g + b


def _mm_kernel(a_ref, w_ref, o_ref):
    o_ref[...] = _dot(a_ref[...], w_ref[...]).astype(o_ref.dtype)


def _matmul(a, w, out_dtype, tm, tn):
    m, k = a.shape
    n = w.shape[1]
    return pl.pallas_call(
        _mm_kernel,
        name="mm",
        grid=(m // tm, n // tn),
        in_specs=[pl.BlockSpec((tm, k), lambda i, j: (i, 0)),
                  pl.BlockSpec((k, tn), lambda i, j: (0, j))],
        out_specs=pl.BlockSpec((tm, tn), lambda i, j: (i, j)),
        out_shape=jax.ShapeDtypeStruct((m, n), out_dtype),
        compiler_params=_cparams(("parallel", "arbitrary")),
    )(a, w)


GLA_TB = 512


def _gla_kernel(q_ref, k_ref, v_ref, r_ref, al_ref, wup_ref, ba_ref, g_ref, o_ref, st_ref):
    c = GLA_CHUNK

    @pl.when(pl.program_id(2) == 0)
    def _():
        st_ref[...] = jnp.zeros_like(st_ref)

    row = lax.broadcasted_iota(jnp.int32, (c, c), 0)
    col = lax.broadcasted_iota(jnp.int32, (c, c), 1)
    causal = row >= col
    tril = causal.astype(F32)
    wup = wup_ref[...]
    ba = ba_ref[...]
    g = g_ref[...]
    scale = GLA_DK ** -0.5

    def chunk(ci, carry):
        sl = pl.ds(pl.multiple_of(ci * c, c), c)
        x = jnp.dot(al_ref[sl, :], wup, preferred_element_type=F32,
                    precision=lax.Precision.HIGHEST) + ba
        log_a = (jnp.minimum(x, 0.0) - jnp.log1p(jnp.exp(-jnp.abs(x)))) / GLA_TAU
        cum = jnp.dot(tril, log_a, preferred_element_type=F32,
                      precision=lax.Precision.HIGHEST)
        last = cum[c - 1:c, :]
        q = q_ref[sl, :].astype(F32) * scale
        k = k_ref[sl, :].astype(F32)
        v = v_ref[sl, :]
        q_in = (q * jnp.exp(cum)).astype(BF16)
        k_in = (k * jnp.exp(-cum)).astype(BF16)
        k_st = (k * jnp.exp(last - cum)).astype(BF16)
        decay = jnp.exp(last)
        att = lax.dot_general(q_in, k_in, (((1,), (1,)), ((), ())),
                              preferred_element_type=F32)
        att = jnp.where(causal, att, 0.0).astype(BF16)
        st = st_ref[...]
        o = _dot(att, v) + lax.dot_general(q_in, st.astype(BF16), (((1,), (1,)), ((), ())),
                                           preferred_element_type=F32)
        st_ref[...] = decay * st + lax.dot_general(v, k_st, (((0,), (0,)), ((), ())),
                                                   preferred_element_type=F32)
        o = o * lax.rsqrt(jnp.mean(o * o, axis=-1, keepdims=True) + LN_EPS) * g
        r = r_ref[sl, :].astype(F32)
        o_ref[sl, :] = (o * (r * _sigmoid(r))).astype(o_ref.dtype)
        return carry

    lax.fori_loop(0, GLA_TB // c, chunk, 0)


def _gla(u, a_low, wup, ba, g_norm, batch, seq):
    n = u.shape[0]
    tb = GLA_TB
    nt = seq // tb
    rows = lambda b, h, t: b * nt + t
    kq = GLA_HEADS * GLA_DK // GLA_DK
    kv = 2 * GLA_HEADS * GLA_DK // GLA_DV
    kr = kv + GLA_HEADS
    return pl.pallas_call(
        _gla_kernel,
        name="gla",
        grid=(batch, GLA_HEADS, nt),
        in_specs=[
            pl.BlockSpec((tb, GLA_DK), lambda b, h, t: (rows(b, h, t), h)),
            pl.BlockSpec((tb, GLA_DK), lambda b, h, t: (rows(b, h, t), kq + h)),
            pl.BlockSpec((tb, GLA_DV), lambda b, h, t: (rows(b, h, t), kv + h)),
            pl.BlockSpec((tb, GLA_DV), lambda b, h, t: (rows(b, h, t), kr + h)),
            pl.BlockSpec((tb, LANES), lambda b, h, t: (rows(b, h, t), 0)),
            pl.BlockSpec((LANES, GLA_DK), lambda b, h, t: (0, h)),
            pl.BlockSpec((1, GLA_DK), lambda b, h, t: (0, h)),
            pl.BlockSpec((1, GLA_DV), lambda b, h, t: (0, 0)),
        ],
        out_specs=pl.BlockSpec((tb, GLA_DV), lambda b, h, t: (rows(b, h, t), h)),
        out_shape=jax.ShapeDtypeStruct((n, GLA_HEADS * GLA_DV), BF16),
        scratch_shapes=[pltpu.VMEM((GLA_DV, GLA_DK), F32)],
        compiler_params=_cparams(("parallel", "parallel", "arbitrary")),
    )(u, u, u, u, a_low, wup, ba, g_norm)


CONV_TT = 256
CONV_HALO = 32
CONV_RB = 64
CONV_LB = 256


def _conv_kernel(val_ref, gate_ref, w_ref, b_ref, lg_ref, lb_ref, o_ref, h_ref, c_ref):
    tt, halo = CONV_TT, CONV_HALO

    @pl.when(pl.program_id(1) == 0)
    def _():
        h_ref[pl.ds(0, halo), :] = jnp.zeros((halo, D_MODEL), F32)

    h_ref[pl.ds(halo, tt), :] = val_ref[...].astype(F32) * _sigmoid(gate_ref[...].astype(F32))

    def lane_block(li, carry):
        ls = pl.ds(pl.multiple_of(li * CONV_LB, CONV_LB), CONV_LB)
        for r0 in range(0, tt, CONV_RB):
            acc = jnp.zeros((CONV_RB, CONV_LB), F32) + b_ref[:, ls]
            for b in range(8):
                win = h_ref[pl.ds(r0 + 8 - b, CONV_RB + 24), ls]
                for a in range(4):
                    d = 8 * a + b
                    if d > CONV_WIDTH - 1:
                        continue
                    j = CONV_WIDTH - 1 - d
                    acc = acc + w_ref[j:j + 1, ls] * win[24 - 8 * a:24 - 8 * a + CONV_RB, :]
            c_ref[pl.ds(r0, CONV_RB), ls] = acc
        return carry

    lax.fori_loop(0, D_MODEL // CONV_LB, lane_block, 0)
    h_ref[pl.ds(0, halo), :] = h_ref[pl.ds(tt, halo), :]
    y = _layer_norm(c_ref[...], lg_ref[...], lb_ref[...])
    o_ref[...] = (y * _sigmoid(y)).astype(o_ref.dtype)


def _conv(u, w_dw, b_dw, ln_g, ln_b, batch, seq):
    n = u.shape[0]
    tt = CONV_TT
    nt = seq // tt
    cv = QKVR // D_MODEL
    vec = pl.BlockSpec((1, D_MODEL), lambda b, t: (0, 0))
    return pl.pallas_call(
        _conv_kernel,
        name="conv",
        grid=(batch, nt),
        in_specs=[
            pl.BlockSpec((tt, D_MODEL), lambda b, t: (b * nt + t, cv)),
            pl.BlockSpec((tt, D_MODEL), lambda b, t: (b * nt + t, cv + 1)),
            pl.BlockSpec((CONV_WIDTH, D_MODEL), lambda b, t: (0, 0)),
            vec, vec, vec,
        ],
        out_specs=pl.BlockSpec((tt, D_MODEL), lambda b, t: (b * nt + t, 0)),
        out_shape=jax.ShapeDtypeStruct((n, D_MODEL), BF16),
        scratch_shapes=[pltpu.VMEM((tt + CONV_HALO, D_MODEL), F32),
                        pltpu.VMEM((tt, D_MODEL), F32)],
        compiler_params=_cparams(("parallel", "arbitrary")),
    )(u, u, w_dw, b_dw, ln_g, ln_b)


MERGE_TM = 256


def _merge_kernel(a1_ref, a2_ref, ga_ref, gb_ref, x_ref, wo_ref, wpw_ref, wout_ref,
                  lg_ref, lb_ref, xo_ref, xb_ref):
    ya = _dot(a1_ref[...], wo_ref[...])
    yb = _dot(a2_ref[...], wpw_ref[...])
    z = _sigmoid(ga_ref[...].astype(F32)) * ya + _sigmoid(gb_ref[...].astype(F32)) * yb
    mix = _dot(z.astype(BF16), wout_ref[...])
    y = _layer_norm(DN_ALPHA * x_ref[...] + mix, lg_ref[...], lb_ref[...])
    xo_ref[...] = y
    xb_ref[...] = y.astype(BF16)


def _merge(a1, a2, u, x, w_o, w_pw, w_out, ln_g, ln_b):
    n = x.shape[0]
    tm = MERGE_TM
    gcol = (QKVR + 2 * D_MODEL) // D_MODEL
    row = lambda c: pl.BlockSpec((tm, D_MODEL), lambda i: (i, c))
    wfull = pl.BlockSpec((D_MODEL, D_MODEL), lambda i: (0, 0), pipeline_mode=pl.Buffered(1))
    vec = pl.BlockSpec((1, D_MODEL), lambda i: (0, 0))
    return pl.pallas_call(
        _merge_kernel,
        name="merge",
        grid=(n // tm,),
        in_specs=[row(0), row(0), row(gcol), row(gcol + 1), row(0), wfull, wfull, wfull, vec, vec],
        out_specs=[row(0), row(0)],
        out_shape=[jax.ShapeDtypeStruct((n, D_MODEL), F32),
                   jax.ShapeDtypeStruct((n, D_MODEL), BF16)],
        compiler_params=_cparams(("parallel",)),
    )(a1, a2, u, u, x, w_o, w_pw, w_out, ln_g, ln_b)


FFN_TM = 512
FFN_TF = 512


def _swiglu_step(xb, wg, wu, wd):
    g = _dot(xb, wg)
    u = _dot(xb, wu)
    h = (g * _sigmoid(g) * u).astype(BF16)
    return _dot(h, wd)


def _ffn_kernel(xb_ref, wg_ref, wu_ref, wd_ref, x_ref, lg_ref, lb_ref, xo_ref, xbo_ref, acc_ref):
    f = pl.program_id(1)
    p = _swiglu_step(xb_ref[...], wg_ref[...], wu_ref[...], wd_ref[...])

    @pl.when(f == 0)
    def _():
        acc_ref[...] = p

    @pl.when(f > 0)
    def _():
        acc_ref[...] += p

    @pl.when(f == pl.num_programs(1) - 1)
    def _():
        y = _layer_norm(DN_ALPHA * x_ref[...] + acc_ref[...], lg_ref[...], lb_ref[...])
        xo_ref[...] = y
        xbo_ref[...] = y.astype(BF16)


def _ffn(xb, x, wg, wu, wd, ln_g, ln_b):
    n = x.shape[0]
    tm, tf = FFN_TM, FFN_TF
    row = pl.BlockSpec((tm, D_MODEL), lambda i, f: (i, 0))
    vec = pl.BlockSpec((1, D_MODEL), lambda i, f: (0, 0))
    return pl.pallas_call(
        _ffn_kernel,
        name="ffn",
        grid=(n // tm, FFN_DIM // tf),
        in_specs=[row,
                  pl.BlockSpec((D_MODEL, tf), lambda i, f: (0, f)),
                  pl.BlockSpec((D_MODEL, tf), lambda i, f: (0, f)),
                  pl.BlockSpec((tf, D_MODEL), lambda i, f: (f, 0)),
                  row, vec, vec],
        out_specs=[row, row],
        out_shape=[jax.ShapeDtypeStruct((n, D_MODEL), F32),
                   jax.ShapeDtypeStruct((n, D_MODEL), BF16)],
        scratch_shapes=[pltpu.VMEM((tm, D_MODEL), F32)],
        compiler_params=_cparams(("parallel", "arbitrary")),
    )(xb, wg, wu, wd, x, ln_g, ln_b)


ROUTER_TM = 1024


def _router_kernel(x_ref, w_ref, idx_ref, wt_ref):
    logits = jnp.dot(x_ref[...], w_ref[...], preferred_element_type=F32,
                     precision=lax.Precision.HIGHEST)
    lane = lax.broadcasted_iota(jnp.int32, logits.shape, 1)
    neg = jnp.float32(-jnp.inf)
    lg = jnp.where(lane < N_EXPERTS, logits, neg)
    m1 = jnp.max(lg, axis=-1, keepdims=True)
    i1 = jnp.min(jnp.where(lg == m1, lane, LANES), axis=-1, keepdims=True)
    lg2 = jnp.where(lane == i1, neg, lg)
    m2 = jnp.max(lg2, axis=-1, keepdims=True)
    i2 = jnp.min(jnp.where(lg2 == m2, lane, LANES), axis=-1, keepdims=True)
    e2 = jnp.exp(m2 - m1)
    w1 = 1.0 / (1.0 + e2)
    w2 = e2 / (1.0 + e2)
    idx_ref[...] = jnp.where(lane == 0, i1, jnp.where(lane == 1, i2, 0))
    wt_ref[...] = jnp.where(lane == 0, w1, jnp.where(lane == 1, w2, 0.0))


def _router(x, w_router_padded):
    n = x.shape[0]
    tm = ROUTER_TM
    out = pl.BlockSpec((tm, LANES), lambda i: (i, 0))
    return pl.pallas_call(
        _router_kernel,
        name="router",
        grid=(n // tm,),
        in_specs=[pl.BlockSpec((tm, D_MODEL), lambda i: (i, 0)),
                  pl.BlockSpec((D_MODEL, LANES), lambda i: (0, 0))],
        out_specs=[out, out],
        out_shape=[jax.ShapeDtypeStruct((n, LANES), jnp.int32),
                   jax.ShapeDtypeStruct((n, LANES), F32)],
        compiler_params=_cparams(("parallel",)),
    )(x, w_router_padded)


MOE_TM = 512
MOE_TF = 512


def _row_copy(src_hbm, dst_vmem, sem, src_row, dst_row):
    return pltpu.make_async_copy(src_hbm.at[pl.ds(src_row, 1), :],
                                 dst_vmem.at[pl.ds(dst_row, 1), :], sem)


def _moe_kernel(be_ref, nu_ref, tok_ref, x_hbm, wg_ref, wu_ref, wd_ref, y_ref,
                xf_ref, xb_ref, acc_ref, sem):
    i = pl.program_id(0)
    f = pl.program_id(1)
    last = pl.num_programs(1) - 1
    used = i < nu_ref[0]

    @pl.when(used & (f == 0))
    def _():
        def issue(r, carry):
            _row_copy(x_hbm, xf_ref, sem, tok_ref[0, r], r).start()
            return carry
        lax.fori_loop(0, MOE_TM, issue, 0)

        def drain(r, carry):
            _row_copy(x_hbm, xf_ref, sem, 0, r).wait()
            return carry
        lax.fori_loop(0, MOE_TM, drain, 0)
        xb_ref[...] = xf_ref[...].astype(BF16)

    @pl.when(used)
    def _():
        p = _swiglu_step(xb_ref[...], wg_ref[...], wu_ref[...], wd_ref[...])

        @pl.when(f == 0)
        def _():
            acc_ref[...] = p

        @pl.when(f > 0)
        def _():
            acc_ref[...] += p

        @pl.when(f == last)
        def _():
            y_ref[...] = acc_ref[...]

    @pl.when(jnp.logical_not(used) & (f == last))
    def _():
        y_ref[...] = jnp.zeros_like(y_ref)


def _moe_ffn(block_e, n_used, row_tok, x, wg, wu, wd):
    n_rows = row_tok.shape[0]
    tm, tf = MOE_TM, MOE_TF
    nb = n_rows // tm
    nf = FFN_DIM // tf
    tok3 = row_tok.reshape(nb, 1, tm)

    def fsel(i, f, nu):
        return jnp.where(i < nu[0], f, nf - 1)

    grid_spec = pltpu.PrefetchScalarGridSpec(
        num_scalar_prefetch=2,
        grid=(nb, nf),
        in_specs=[
            pl.BlockSpec((None, 1, tm), lambda i, f, be, nu: (i, 0, 0), memory_space=pltpu.SMEM),
            pl.BlockSpec(memory_space=pl.ANY),
            pl.BlockSpec((None, D_MODEL, tf), lambda i, f, be, nu: (be[i], 0, fsel(i, f, nu))),
            pl.BlockSpec((None, D_MODEL, tf), lambda i, f, be, nu: (be[i], 0, fsel(i, f, nu))),
            pl.BlockSpec((None, tf, D_MODEL), lambda i, f, be, nu: (be[i], fsel(i, f, nu), 0)),
        ],
        out_specs=pl.BlockSpec((tm, D_MODEL), lambda i, f, be, nu: (i, 0)),
        scratch_shapes=[pltpu.VMEM((tm, D_MODEL), F32),
                        pltpu.VMEM((tm, D_MODEL), BF16),
                        pltpu.VMEM((tm, D_MODEL), F32),
                        pltpu.SemaphoreType.DMA(())],
    )
    return pl.pallas_call(
        _moe_kernel,
        name="moe_ffn",
        grid_spec=grid_spec,
        out_shape=jax.ShapeDtypeStruct((n_rows, D_MODEL), F32),
        compiler_params=_cparams(("arbitrary", "arbitrary")),
    )(block_e, n_used, tok3, x, wg, wu, wd)


COMB_TT = 256


def _combine_kernel(pos_ref, y_hbm, wt_ref, x_ref, lg_ref, lb_ref, xo_ref, xbo_ref,
                    g0_ref, g1_ref, sem):
    tt = COMB_TT

    def issue(r, carry):
        _row_copy(y_hbm, g0_ref, sem, pos_ref[0, 2 * r], r).start()
        _row_copy(y_hbm, g1_ref, sem, pos_ref[0, 2 * r + 1], r).start()
        return carry
    lax.fori_loop(0, tt, issue, 0)

    def drain(r, carry):
        _row_copy(y_hbm, g0_ref, sem, 0, r).wait()
        _row_copy(y_hbm, g1_ref, sem, 0, r).wait()
        return carry
    lax.fori_loop(0, tt, drain, 0)

    wt = wt_ref[...]
    f = wt[:, 0:1] * g0_ref[...] + wt[:, 1:2] * g1_ref[...]
    y = _layer_norm(DN_ALPHA * x_ref[...] + f, lg_ref[...], lb_ref[...])
    xo_ref[...] = y
    xbo_ref[...] = y.astype(BF16)


def _combine(pos, yrows, wt, x, ln_g, ln_b):
    n = x.shape[0]
    tt = COMB_TT
    pos3 = pos.reshape(n // tt, 1, TOP_K * tt)
    row = pl.BlockSpec((tt, D_MODEL), lambda i: (i, 0))
    vec = pl.BlockSpec((1, D_MODEL), lambda i: (0, 0))
    return pl.pallas_call(
        _combine_kernel,
        name="combine",
        grid=(n // tt,),
        in_specs=[
            pl.BlockSpec((None, 1, TOP_K * tt), lambda i: (i, 0, 0), memory_space=pltpu.SMEM),
            pl.BlockSpec(memory_space=pl.ANY),
            pl.BlockSpec((tt, LANES), lambda i: (i, 0)),
            row, vec, vec,
        ],
        out_specs=[row, row],
        out_shape=[jax.ShapeDtypeStruct((n, D_MODEL), F32),
                   jax.ShapeDtypeStruct((n, D_MODEL), BF16)],
        scratch_shapes=[pltpu.VMEM((tt, D_MODEL), F32),
                        pltpu.VMEM((tt, D_MODEL), F32),
                        pltpu.SemaphoreType.DMA(())],
        compiler_params=_cparams(("arbitrary",)),
    )(pos3, yrows, wt, x, ln_g, ln_b)


def _routing_plan(idx, n):
    n_assign = n * TOP_K
    e_flat = idx[:, :TOP_K].reshape(-1)
    onehot = (e_flat[:, None] == jnp.arange(N_EXPERTS, dtype=jnp.int32)[None, :]).astype(jnp.int32)
    csum = jnp.cumsum(onehot, axis=0)
    rank = jnp.take_along_axis(csum, e_flat[:, None], axis=1)[:, 0] - 1
    counts = csum[-1]
    padded = (counts + MOE_TM - 1) // MOE_TM * MOE_TM
    pend = jnp.cumsum(padded)
    pstart = pend - padded
    pos = (pstart[e_flat] + rank).astype(jnp.int32)
    n_rows = (-(-n_assign // MOE_TM) + N_EXPERTS) * MOE_TM
    nb = n_rows // MOE_TM
    row_tok = jnp.zeros((n_rows,), jnp.int32).at[pos].set(
        jnp.arange(n_assign, dtype=jnp.int32) // TOP_K)
    block_e = jnp.minimum(
        jnp.searchsorted(pend, jnp.arange(nb, dtype=jnp.int32) * MOE_TM, side='right'),
        N_EXPERTS - 1).astype(jnp.int32)
    n_used = (pend[-1:] // MOE_TM).astype(jnp.int32)
    return pos, row_tok, block_e, n_used


def kernel(x, w_in, gla_a_up, gla_a_bias, gla_norm_g, gla_w_o, conv_w_dw, conv_b_dw, conv_ln_g, conv_ln_b, conv_w_pw, w_out, ln_g, ln_b, ffn_w_gate, ffn_w_up, ffn_w_down, moe_w_router, moe_w_gate, moe_w_up, moe_w_down):
    batch, seq, d = x.shape
    n = batch * seq
    xf = x.reshape(n, d)
    xb = xf.astype(BF16)
    a_lo, a_hi = QKVR, QKVR + GLA_RANK
    for layer in range(DEPTH):
        w_main = jnp.concatenate([w_in[layer, :, :a_lo], w_in[layer, :, a_hi:]], axis=1).astype(BF16)
        w_alow = jnp.pad(w_in[layer, :, a_lo:a_hi], ((0, 0), (0, LANES - GLA_RANK))).astype(BF16)
        wup = jnp.pad(gla_a_up[layer], ((0, LANES - GLA_RANK), (0, 0)))
        u = _matmul(xb, w_main, BF16, 1024, 1024)
        a_low = _matmul(xb, w_alow, F32, 1024, LANES)
        a1 = _gla(u, a_low, wup, gla_a_bias[layer][None, :], gla_norm_g[layer][None, :], batch, seq)
        a2 = _conv(u, conv_w_dw[layer], conv_b_dw[layer][None, :], conv_ln_g[layer][None, :],
                   conv_ln_b[layer][None, :], batch, seq)
        xf, xb = _merge(a1, a2, u, xf, gla_w_o[layer].astype(BF16), conv_w_pw[layer].astype(BF16),
                        w_out[layer].astype(BF16), ln_g[layer, 0][None, :], ln_b[layer, 0][None, :])
        i = layer // 2
        g2, b2 = ln_g[layer, 1][None, :], ln_b[layer, 1][None, :]
        if layer % 2 == 0:
            xf, xb = _ffn(xb, xf, ffn_w_gate[i].astype(BF16), ffn_w_up[i].astype(BF16),
                          ffn_w_down[i].astype(BF16), g2, b2)
        else:
            w_r = jnp.pad(moe_w_router[i], ((0, 0), (0, LANES - N_EXPERTS)))
            idx, wt = _router(xf, w_r)
            pos, row_tok, block_e, n_used = _routing_plan(idx, n)
            yrows = _moe_ffn(block_e, n_used, row_tok, xf, moe_w_gate[i].astype(BF16),
                             moe_w_up[i].astype(BF16), moe_w_down[i].astype(BF16))
            xf, xb = _combine(pos, yrows, wt, xf, g2, b2)
    return xf.reshape(batch, seq, d)
```

```python
import jax
import jax.numpy as jnp
from jax import lax
from jax.experimental import pallas as pl
from jax.experimental.pallas import tpu as pltpu

D_MODEL = 2048
DEPTH = 4
GLA_HEADS = 4
GLA_DK = 256
GLA_DV = 512
GLA_RANK = 16
GLA_TAU = 16.0
GLA_CHUNK = 64
CONV_WIDTH = 31
FFN_DIM = 5632
N_EXPERTS = 8
TOP_K = 2
DN_ALPHA = (2 * DEPTH) ** 0.25
LN_EPS = 1e-5

LANES = 128
SUBLANES = 8
QKVR = 2 * GLA_HEADS * GLA_DK + 2 * GLA_HEADS * GLA_DV
U_MAIN = QKVR + 4 * D_MODEL
VMEM_LIMIT = 56 * 1024 * 1024

F32 = jnp.float32
BF16 = jnp.bfloat16
NT = (((1,), (1,)), ((), ()))
TN = (((0,), (0,)), ((), ()))


def _cparams(sem):
    return pltpu.CompilerParams(dimension_semantics=sem, vmem_limit_bytes=VMEM_LIMIT)


def _dot(a, b):
    return jnp.dot(a, b, preferred_element_type=F32)


def _dot_f32(a, b):
    return jnp.dot(a, b, preferred_element_type=F32, precision=lax.Precision.HIGHEST)


def _split_bf16(a):
    hi = a.astype(BF16)
    lo = (a - hi.astype(F32)).astype(BF16)
    return hi, lo


def _dot_3pass(a, b):
    a_hi, a_lo = _split_bf16(a)
    b_hi, b_lo = _split_bf16(b)
    return _dot(a_hi, b_hi) + (_dot(a_hi, b_lo) + _dot(a_lo, b_hi))


def _dot_exact_lhs(a_exact, b):
    b_hi, b_lo = _split_bf16(b)
    return _dot(a_exact, b_hi) + _dot(a_exact, b_lo)


def _sigmoid(x):
    return 1.0 / (1.0 + jnp.exp(-x))


def _layer_norm(y, g, b):
    mu = jnp.mean(y, axis=-1, keepdims=True)
    d = y - mu
    var = jnp.mean(d * d, axis=-1, keepdims=True)
    return d * lax.rsqrt(var + LN_EPS) * g + b


def _mm_kernel(a_ref, w_ref, o_ref):
    o_ref[...] = _dot(a_ref[...], w_ref[...]).astype(o_ref.dtype)


def _matmul(a, w_stack, layer, out_dtype, tm, tn):
    m, k = a.shape
    n = w_stack.shape[2]
    return pl.pallas_call(
        _mm_kernel,
        name="mm",
        grid=(m // tm, n // tn),
        in_specs=[pl.BlockSpec((tm, k), lambda i, j: (i, 0)),
                  pl.BlockSpec((None, k, tn), lambda i, j: (layer, 0, j))],
        out_specs=pl.BlockSpec((tm, tn), lambda i, j: (i, j)),
        out_shape=jax.ShapeDtypeStruct((m, n), out_dtype),
        compiler_params=_cparams(("parallel", "arbitrary")),
    )(a, w_stack)


GLA_TB = 256


def _gla_kernel(q_ref, k_ref, v_ref, r_ref, al_ref, wup_ref, ba_ref, g_ref, o_ref, st_ref):
    c, tb = GLA_CHUNK, GLA_TB
    nc = tb // c

    @pl.when(pl.program_id(2) == 0)
    def _():
        st_ref[...] = jnp.zeros_like(st_ref)

    row = lax.broadcasted_iota(jnp.int32, (tb, tb), 0)
    col = lax.broadcasted_iota(jnp.int32, (tb, tb), 1)
    shift = c.bit_length() - 1
    same_chunk = lax.shift_right_logical(row, shift) == lax.shift_right_logical(col, shift)
    tril_blocks = (same_chunk & (row >= col)).astype(BF16)
    causal = (lax.broadcasted_iota(jnp.int32, (c, c), 0)
              >= lax.broadcasted_iota(jnp.int32, (c, c), 1))

    x = _dot_3pass(al_ref[...], wup_ref[...]) + ba_ref[...]
    log_a = (jnp.minimum(x, 0.0) - jnp.log1p(jnp.exp(-jnp.abs(x)))) / GLA_TAU
    cum = _dot_exact_lhs(tril_blocks, log_a)
    last = [cum[(ci + 1) * c - 1:(ci + 1) * c, :] for ci in range(nc)]
    last_rows = jnp.concatenate([jnp.broadcast_to(l, (c, GLA_DK)) for l in last], axis=0)
    q = q_ref[...].astype(F32) * (GLA_DK ** -0.5)
    k = k_ref[...].astype(F32)
    q_in = (q * jnp.exp(cum)).astype(BF16)
    k_in = (k * jnp.exp(-cum)).astype(BF16)
    k_st = (k * jnp.exp(last_rows - cum)).astype(BF16)
    v = v_ref[...]

    outs = []
    for ci in range(nc):
        sl = slice(ci * c, (ci + 1) * c)
        att = lax.dot_general(q_in[sl], k_in[sl], NT, preferred_element_type=F32)
        att = jnp.where(causal, att, 0.0).astype(BF16)
        st = st_ref[...]
        outs.append(_dot(att, v[sl])
                    + lax.dot_general(q_in[sl], st.astype(BF16), NT, preferred_element_type=F32))
        st_ref[...] = jnp.exp(last[ci]) * st + lax.dot_general(
            v[sl], k_st[sl], TN, preferred_element_type=F32)
    o = jnp.concatenate(outs, axis=0)
    o = o * lax.rsqrt(jnp.mean(o * o, axis=-1, keepdims=True) + LN_EPS) * g_ref[...]
    r = r_ref[...].astype(F32)
    o_ref[...] = (o * (r * _sigmoid(r))).astype(o_ref.dtype)


def _gla(u, a_low, wup, ba, g_norm, layer, batch, seq):
    n = u.shape[0]
    tb = GLA_TB
    nt = seq // tb
    rows = lambda b, h, t: b * nt + t
    kq = GLA_HEADS
    kv = 2 * GLA_HEADS * GLA_DK // GLA_DV
    kr = kv + GLA_HEADS
    return pl.pallas_call(
        _gla_kernel,
        name="gla",
        grid=(batch, GLA_HEADS, nt),
        in_specs=[
            pl.BlockSpec((tb, GLA_DK), lambda b, h, t: (rows(b, h, t), h)),
            pl.BlockSpec((tb, GLA_DK), lambda b, h, t: (rows(b, h, t), kq + h)),
            pl.BlockSpec((tb, GLA_DV), lambda b, h, t: (rows(b, h, t), kv + h)),
            pl.BlockSpec((tb, GLA_DV), lambda b, h, t: (rows(b, h, t), kr + h)),
            pl.BlockSpec((tb, LANES), lambda b, h, t: (rows(b, h, t), 0)),
            pl.BlockSpec((None, LANES, GLA_DK), lambda b, h, t: (layer, 0, h)),
            pl.BlockSpec((None, 1, GLA_DK), lambda b, h, t: (layer, 0, h)),
            pl.BlockSpec((None, 1, GLA_DV), lambda b, h, t: (layer, 0, 0)),
        ],
        out_specs=pl.BlockSpec((tb, GLA_DV), lambda b, h, t: (rows(b, h, t), h)),
        out_shape=jax.ShapeDtypeStruct((n, GLA_HEADS * GLA_DV), BF16),
        scratch_shapes=[pltpu.VMEM((GLA_DV, GLA_DK), F32)],
        compiler_params=_cparams(("parallel", "parallel", "arbitrary")),
    )(u, u, u, u, a_low, wup, ba, g_norm)


CONV_TT = 256
CONV_HALO = 32
CONV_RB = 128


def _conv_kernel(val_ref, gate_ref, w_ref, b_ref, lg_ref, lb_ref, o_ref, h_ref, c_ref):
    tt, halo, rb = CONV_TT, CONV_HALO, CONV_RB

    @pl.when(pl.program_id(1) == 0)
    def _():
        h_ref[pl.ds(0, halo), :] = jnp.zeros((halo, D_MODEL), F32)

    h_ref[pl.ds(halo, tt), :] = val_ref[...].astype(F32) * _sigmoid(gate_ref[...].astype(F32))

    def lane_block(li, carry):
        ls = pl.ds(pl.multiple_of(li * LANES, LANES), LANES)
        for r0 in range(0, tt, rb):
            win = h_ref[pl.ds(r0, rb + halo), ls]
            acc = jnp.broadcast_to(b_ref[:, ls], (rb, LANES))
            for b in range(SUBLANES):
                shifted = win if b == 0 else pltpu.roll(win, b, axis=0)
                for a in range(halo // SUBLANES):
                    d = SUBLANES * a + b
                    if d > CONV_WIDTH - 1:
                        continue
                    j = CONV_WIDTH - 1 - d
                    lo = halo - SUBLANES * a
                    acc = acc + w_ref[j:j + 1, ls] * shifted[lo:lo + rb, :]
            c_ref[pl.ds(r0, rb), ls] = acc
        return carry

    lax.fori_loop(0, D_MODEL // LANES, lane_block, 0)
    h_ref[pl.ds(0, halo), :] = h_ref[pl.ds(tt, halo), :]
    y = _layer_norm(c_ref[...], lg_ref[...], lb_ref[...])
    o_ref[...] = (y * _sigmoid(y)).astype(o_ref.dtype)


def _conv(u, w_dw, b_dw, ln_g, ln_b, layer, batch, seq):
    n = u.shape[0]
    tt = CONV_TT
    nt = seq // tt
    cv = QKVR // D_MODEL
    vec = pl.BlockSpec((None, 1, D_MODEL), lambda b, t: (layer, 0, 0))
    return pl.pallas_call(
        _conv_kernel,
        name="conv",
        grid=(batch, nt),
        in_specs=[
            pl.BlockSpec((tt, D_MODEL), lambda b, t: (b * nt + t, cv)),
            pl.BlockSpec((tt, D_MODEL), lambda b, t: (b * nt + t, cv + 1)),
            pl.BlockSpec((None, CONV_WIDTH, D_MODEL), lambda b, t: (layer, 0, 0)),
            vec, vec, vec,
        ],
        out_specs=pl.BlockSpec((tt, D_MODEL), lambda b, t: (b * nt + t, 0)),
        out_shape=jax.ShapeDtypeStruct((n, D_MODEL), BF16),
        scratch_shapes=[pltpu.VMEM((tt + CONV_HALO, D_MODEL), F32),
                        pltpu.VMEM((tt, D_MODEL), F32)],
        compiler_params=_cparams(("parallel", "arbitrary")),
    )(u, u, w_dw, b_dw, ln_g, ln_b)


MERGE_TM = 256


def _merge_kernel(a1_ref, a2_ref, ga_ref, gb_ref, x_ref, wo_ref, wpw_ref, wout_ref,
                  lg_ref, lb_ref, xo_ref, xb_ref):
    ya = _dot(a1_ref[...], wo_ref[...])
    yb = _dot(a2_ref[...], wpw_ref[...])
    z = _sigmoid(ga_ref[...].astype(F32)) * ya + _sigmoid(gb_ref[...].astype(F32)) * yb
    mix = _dot(z.astype(BF16), wout_ref[...])
    y = _layer_norm(DN_ALPHA * x_ref[...] + mix, lg_ref[...], lb_ref[...])
    xo_ref[...] = y
    xb_ref[...] = y.astype(BF16)


def _merge(a1, a2, u, x, w_o, w_pw, w_out, ln_g, ln_b, layer):
    n = x.shape[0]
    tm = MERGE_TM
    gcol = (QKVR + 2 * D_MODEL) // D_MODEL
    row = lambda c: pl.BlockSpec((tm, D_MODEL), lambda i: (i, c))
    wfull = pl.BlockSpec((None, D_MODEL, D_MODEL), lambda i: (layer, 0, 0),
                         pipeline_mode=pl.Buffered(1))
    vec = pl.BlockSpec((None, 1, D_MODEL), lambda i: (2 * layer, 0, 0))
    return pl.pallas_call(
        _merge_kernel,
        name="merge",
        grid=(n // tm,),
        in_specs=[row(0), row(0), row(gcol), row(gcol + 1), row(0), wfull, wfull, wfull, vec, vec],
        out_specs=[row(0), row(0)],
        out_shape=[jax.ShapeDtypeStruct((n, D_MODEL), F32),
                   jax.ShapeDtypeStruct((n, D_MODEL), BF16)],
        compiler_params=_cparams(("parallel",)),
    )(a1, a2, u, u, x, w_o, w_pw, w_out, ln_g, ln_b)


FFN_TM = 512
FFN_TF = 512
FFN_SPLIT = 2


def _swiglu_accumulate(xb_ref, wg_ref, wu_ref, wd_ref, acc_ref, tm):
    rows = tm // FFN_SPLIT
    wg, wu, wd = wg_ref[...], wu_ref[...], wd_ref[...]
    for s in range(FFN_SPLIT):
        sl = pl.ds(s * rows, rows)
        xb = xb_ref[sl, :]
        g = _dot(xb, wg)
        u = _dot(xb, wu)
        h = (g * _sigmoid(g) * u).astype(BF16)
        acc_ref[sl, :] += _dot(h, wd)


def _ffn_kernel(xb_ref, wg_ref, wu_ref, wd_ref, x_ref, lg_ref, lb_ref, xo_ref, xbo_ref, acc_ref):
    f = pl.program_id(1)

    @pl.when(f == 0)
    def _():
        acc_ref[...] = jnp.zeros_like(acc_ref)

    _swiglu_accumulate(xb_ref, wg_ref, wu_ref, wd_ref, acc_ref, FFN_TM)

    @pl.when(f == pl.num_programs(1) - 1)
    def _():
        y = _layer_norm(DN_ALPHA * x_ref[...] + acc_ref[...], lg_ref[...], lb_ref[...])
        xo_ref[...] = y
        xbo_ref[...] = y.astype(BF16)


def _ffn(xb, x, wg, wu, wd, ln_g, ln_b, layer):
    n = x.shape[0]
    tm, tf = FFN_TM, FFN_TF
    il = layer // 2
    row = pl.BlockSpec((tm, D_MODEL), lambda i, f: (i, 0))
    vec = pl.BlockSpec((None, 1, D_MODEL), lambda i, f: (2 * layer + 1, 0, 0))
    return pl.pallas_call(
        _ffn_kernel,
        name="ffn",
        grid=(n // tm, FFN_DIM // tf),
        in_specs=[row,
                  pl.BlockSpec((None, D_MODEL, tf), lambda i, f: (il, 0, f)),
                  pl.BlockSpec((None, D_MODEL, tf), lambda i, f: (il, 0, f)),
                  pl.BlockSpec((None, tf, D_MODEL), lambda i, f: (il, f, 0)),
                  row, vec, vec],
        out_specs=[row, row],
        out_shape=[jax.ShapeDtypeStruct((n, D_MODEL), F32),
                   jax.ShapeDtypeStruct((n, D_MODEL), BF16)],
        scratch_shapes=[pltpu.VMEM((tm, D_MODEL), F32)],
        compiler_params=_cparams(("parallel", "arbitrary")),
    )(xb, wg, wu, wd, x, ln_g, ln_b)


ROUTER_TM = 1024


def _router_kernel(x_ref, w_ref, idx_ref, wt_ref):
    logits = _dot_f32(x_ref[...], w_ref[...])
    lane = lax.broadcasted_iota(jnp.int32, logits.shape, 1)
    neg = jnp.float32(-jnp.inf)
    lg = jnp.where(lane < N_EXPERTS, logits, neg)
    m1 = jnp.max(lg, axis=-1, keepdims=True)
    i1 = jnp.min(jnp.where(lg == m1, lane, LANES), axis=-1, keepdims=True)
    lg2 = jnp.where(lane == i1, neg, lg)
    m2 = jnp.max(lg2, axis=-1, keepdims=True)
    i2 = jnp.min(jnp.where(lg2 == m2, lane, LANES), axis=-1, keepdims=True)
    e2 = jnp.exp(m2 - m1)
    w1 = 1.0 / (1.0 + e2)
    w2 = e2 / (1.0 + e2)
    idx_ref[...] = jnp.where(lane == 0, i1, jnp.where(lane == 1, i2, 0))
    wt_ref[...] = jnp.where(lane == 0, w1, jnp.where(lane == 1, w2, 0.0))


def _router(x, w_router_padded, layer):
    n = x.shape[0]
    tm = ROUTER_TM
    out = pl.BlockSpec((tm, LANES), lambda i: (i, 0))
    return pl.pallas_call(
        _router_kernel,
        name="router",
        grid=(n // tm,),
        in_specs=[pl.BlockSpec((tm, D_MODEL), lambda i: (i, 0)),
                  pl.BlockSpec((None, D_MODEL, LANES), lambda i: (layer // 2, 0, 0))],
        out_specs=[out, out],
        out_shape=[jax.ShapeDtypeStruct((n, LANES), jnp.int32),
                   jax.ShapeDtypeStruct((n, LANES), F32)],
        compiler_params=_cparams(("parallel",)),
    )(x, w_router_padded)


MOE_TM = 512
MOE_TF = 512
GATHER_UNROLL = 8


def _row_copy(src_hbm, dst_vmem, sem, src_row, dst_row):
    return pltpu.make_async_copy(src_hbm.at[pl.ds(src_row, 1), :],
                                 dst_vmem.at[pl.ds(dst_row, 1), :], sem)


def _gather_rows(src_hbm, idx_ref, dst_vmem, sem, n_rows):
    def issue(r, carry):
        _row_copy(src_hbm, dst_vmem, sem, idx_ref[0, r], r).start()
        return carry
    lax.fori_loop(0, n_rows, issue, 0, unroll=GATHER_UNROLL)


def _wait_rows(src_hbm, dst_vmem, sem, n_rows):
    def drain(r, carry):
        _row_copy(src_hbm, dst_vmem, sem, 0, r).wait()
        return carry
    lax.fori_loop(0, n_rows, drain, 0, unroll=GATHER_UNROLL)


def _moe_kernel(be_ref, nu_ref, tok_ref, tokn_ref, x_hbm, wg_ref, wu_ref, wd_ref, y_ref,
                xf_ref, xb_ref, acc_ref, sem):
    i = pl.program_id(0)
    f = pl.program_id(1)
    last = pl.num_programs(1) - 1
    n_used = nu_ref[0]
    used = i < n_used
    slot = lax.rem(i, 2)

    @pl.when(used & (f == 0))
    def _():
        @pl.when(i == 0)
        def _():
            _gather_rows(x_hbm, tok_ref, xf_ref.at[0], sem.at[0], MOE_TM)

        _wait_rows(x_hbm, xf_ref.at[slot], sem.at[slot], MOE_TM)
        xb_ref[...] = xf_ref[slot].astype(BF16)
        acc_ref[...] = jnp.zeros_like(acc_ref)

        @pl.when(i + 1 < n_used)
        def _():
            _gather_rows(x_hbm, tokn_ref, xf_ref.at[1 - slot], sem.at[1 - slot], MOE_TM)

    @pl.when(used)
    def _():
        _swiglu_accumulate(xb_ref, wg_ref, wu_ref, wd_ref, acc_ref, MOE_TM)

    @pl.when(f == last)
    def _():
        @pl.when(used)
        def _():
            y_ref[...] = acc_ref[...]

        @pl.when(jnp.logical_not(used))
        def _():
            y_ref[...] = jnp.zeros_like(y_ref)


def _moe_ffn(block_e, n_used, row_tok, x, wg, wu, wd, layer):
    n_rows = row_tok.shape[0]
    tm, tf = MOE_TM, MOE_TF
    nb = n_rows // tm
    nf = FFN_DIM // tf
    il = layer // 2
    tok3 = row_tok.reshape(nb, 1, tm)

    def fsel(i, f, nu):
        return jnp.where(i < nu[0], f, nf - 1)

    grid_spec = pltpu.PrefetchScalarGridSpec(
        num_scalar_prefetch=2,
        grid=(nb, nf),
        in_specs=[
            pl.BlockSpec((None, 1, tm), lambda i, f, be, nu: (i, 0, 0), memory_space=pltpu.SMEM),
            pl.BlockSpec((None, 1, tm), lambda i, f, be, nu: (jnp.minimum(i + 1, nb - 1), 0, 0),
                         memory_space=pltpu.SMEM),
            pl.BlockSpec(memory_space=pl.ANY),
            pl.BlockSpec((None, None, D_MODEL, tf),
                         lambda i, f, be, nu: (il, be[i], 0, fsel(i, f, nu))),
            pl.BlockSpec((None, None, D_MODEL, tf),
                         lambda i, f, be, nu: (il, be[i], 0, fsel(i, f, nu))),
            pl.BlockSpec((None, None, tf, D_MODEL),
                         lambda i, f, be, nu: (il, be[i], fsel(i, f, nu), 0)),
        ],
        out_specs=pl.BlockSpec((tm, D_MODEL), lambda i, f, be, nu: (i, 0)),
        scratch_shapes=[pltpu.VMEM((2, tm, D_MODEL), F32),
                        pltpu.VMEM((tm, D_MODEL), BF16),
                        pltpu.VMEM((tm, D_MODEL), F32),
                        pltpu.SemaphoreType.DMA((2,))],
    )
    return pl.pallas_call(
        _moe_kernel,
        name="moe_ffn",
        grid_spec=grid_spec,
        out_shape=jax.ShapeDtypeStruct((n_rows, D_MODEL), F32),
        compiler_params=_cparams(("arbitrary", "arbitrary")),
    )(block_e, n_used, tok3, tok3, x, wg, wu, wd)


COMB_TT = 256


def _combine_kernel(pos0_ref, pos1_ref, y_hbm, wt_ref, x_ref, lg_ref, lb_ref, xo_ref, xbo_ref,
                    g0_ref, g1_ref, sem):
    tt = COMB_TT
    _gather_rows(y_hbm, pos0_ref, g0_ref, sem.at[0], tt)
    _gather_rows(y_hbm, pos1_ref, g1_ref, sem.at[1], tt)
    _wait_rows(y_hbm, g0_ref, sem.at[0], tt)
    _wait_rows(y_hbm, g1_ref, sem.at[1], tt)
    wt = wt_ref[...]
    f = wt[:, 0:1] * g0_ref[...] + wt[:, 1:2] * g1_ref[...]
    y = _layer_norm(DN_ALPHA * x_ref[...] + f, lg_ref[...], lb_ref[...])
    xo_ref[...] = y
    xbo_ref[...] = y.astype(BF16)


def _combine(pos, yrows, wt, x, ln_g, ln_b, layer):
    n = x.shape[0]
    tt = COMB_TT
    pos0 = pos[0::TOP_K].reshape(n // tt, 1, tt)
    pos1 = pos[1::TOP_K].reshape(n // tt, 1, tt)
    row = pl.BlockSpec((tt, D_MODEL), lambda i: (i, 0))
    vec = pl.BlockSpec((None, 1, D_MODEL), lambda i: (2 * layer + 1, 0, 0))
    idx = pl.BlockSpec((None, 1, tt), lambda i: (i, 0, 0), memory_space=pltpu.SMEM)
    return pl.pallas_call(
        _combine_kernel,
        name="combine",
        grid=(n // tt,),
        in_specs=[idx, idx,
                  pl.BlockSpec(memory_space=pl.ANY),
                  pl.BlockSpec((tt, LANES), lambda i: (i, 0)),
                  row, vec, vec],
        out_specs=[row, row],
        out_shape=[jax.ShapeDtypeStruct((n, D_MODEL), F32),
                   jax.ShapeDtypeStruct((n, D_MODEL), BF16)],
        scratch_shapes=[pltpu.VMEM((tt, D_MODEL), F32),
                        pltpu.VMEM((tt, D_MODEL), F32),
                        pltpu.SemaphoreType.DMA((2,))],
        compiler_params=_cparams(("arbitrary",)),
    )(pos0, pos1, yrows, wt, x, ln_g, ln_b)


def _routing_plan(idx, n):
    n_assign = n * TOP_K
    e_flat = idx[:, :TOP_K].reshape(-1)
    onehot = (e_flat[:, None] == jnp.arange(N_EXPERTS, dtype=jnp.int32)[None, :]).astype(jnp.int32)
    csum = jnp.cumsum(onehot, axis=0)
    rank = jnp.sum(csum * onehot, axis=1) - 1
    counts = csum[-1]
    padded = (counts + MOE_TM - 1) // MOE_TM * MOE_TM
    pend = jnp.cumsum(padded)
    pstart = pend - padded
    pos = (jnp.sum(pstart[None, :] * onehot, axis=1) + rank).astype(jnp.int32)
    n_rows = (-(-n_assign // MOE_TM) + N_EXPERTS) * MOE_TM
    nb = n_rows // MOE_TM
    row_tok = jnp.zeros((n_rows,), jnp.int32).at[pos].set(
        jnp.arange(n_assign, dtype=jnp.int32) // TOP_K)
    block_start = jnp.arange(nb, dtype=jnp.int32) * MOE_TM
    block_e = jnp.minimum(jnp.sum((pend[None, :] <= block_start[:, None]).astype(jnp.int32), axis=1),
                          N_EXPERTS - 1).astype(jnp.int32)
    n_used = (pend[-1:] // MOE_TM).astype(jnp.int32)
    return pos, row_tok, block_e, n_used


def kernel(x, w_in, gla_a_up, gla_a_bias, gla_norm_g, gla_w_o, conv_w_dw, conv_b_dw, conv_ln_g, conv_ln_b, conv_w_pw, w_out, ln_g, ln_b, ffn_w_gate, ffn_w_up, ffn_w_down, moe_w_router, moe_w_gate, moe_w_up, moe_w_down):
    batch, seq, d = x.shape
    n = batch * seq
    xf = x.reshape(n, d)
    xb = xf.astype(BF16)
    a_lo, a_hi = QKVR, QKVR + GLA_RANK
    w_main = jnp.concatenate([w_in[:, :, :a_lo], w_in[:, :, a_hi:]], axis=2).astype(BF16)
    w_alow = jnp.pad(w_in[:, :, a_lo:a_hi], ((0, 0), (0, 0), (0, LANES - GLA_RANK))).astype(BF16)
    wup = jnp.pad(gla_a_up, ((0, 0), (0, LANES - GLA_RANK), (0, 0)))
    w_o, w_pw, w_mix = gla_w_o.astype(BF16), conv_w_pw.astype(BF16), w_out.astype(BF16)
    ffn_g, ffn_u, ffn_d = ffn_w_gate.astype(BF16), ffn_w_up.astype(BF16), ffn_w_down.astype(BF16)
    moe_g, moe_u, moe_d = moe_w_gate.astype(BF16), moe_w_up.astype(BF16), moe_w_down.astype(BF16)
    w_r = jnp.pad(moe_w_router, ((0, 0), (0, 0), (0, LANES - N_EXPERTS)))
    ba, gn = gla_a_bias[:, None, :], gla_norm_g[:, None, :]
    cb, cg, cbeta = conv_b_dw[:, None, :], conv_ln_g[:, None, :], conv_ln_b[:, None, :]
    lng = ln_g.reshape(2 * DEPTH, 1, d)
    lnb = ln_b.reshape(2 * DEPTH, 1, d)
    for layer in range(DEPTH):
        u = _matmul(xb, w_main, layer, BF16, 1024, 1024)
        a_low = _matmul(xb, w_alow, layer, F32, 1024, LANES)
        a1 = _gla(u, a_low, wup, ba, gn, layer, batch, seq)
        a2 = _conv(u, conv_w_dw, cb, cg, cbeta, layer, batch, seq)
        xf, xb = _merge(a1, a2, u, xf, w_o, w_pw, w_mix, lng, lnb, layer)
        if layer % 2 == 0:
            xf, xb = _ffn(xb, xf, ffn_g, ffn_u, ffn_d, lng, lnb, layer)
        else:
            idx, wt = _router(xf, w_r, layer)
            pos, row_tok, block_e, n_used = _routing_plan(idx, n)
            yrows = _moe_ffn(block_e, n_used, row_tok, xf, moe_g, moe_u, moe_d, layer)
            xf, xb = _combine(pos, yrows, wt, xf, lng, lnb, layer)
    return xf.reshape(batch, seq, d)
```

```python
import jax
import jax.numpy as jnp
from jax import lax
from jax.experimental import pallas as pl
from jax.experimental.pallas import tpu as pltpu

D_MODEL = 2048
DEPTH = 4
GLA_HEADS = 4
GLA_DK = 256
GLA_DV = 512
GLA_RANK = 16
GLA_TAU = 16.0
GLA_CHUNK = 64
CONV_WIDTH = 31
FFN_DIM = 5632
N_EXPERTS = 8
TOP_K = 2
DN_ALPHA = (2 * DEPTH) ** 0.25
LN_EPS = 1e-5

LANES = 128
SUBLANES = 8
QKVR = 2 * GLA_HEADS * GLA_DK + 2 * GLA_HEADS * GLA_DV
VMEM_LIMIT = 56 * 1024 * 1024

F32 = jnp.float32
BF16 = jnp.bfloat16
NT = (((1,), (1,)), ((), ()))
TN = (((0,), (0,)), ((), ()))


def _cparams(sem):
    return pltpu.CompilerParams(dimension_semantics=sem, vmem_limit_bytes=VMEM_LIMIT)


def _dot(a, b):
    return jnp.dot(a, b, preferred_element_type=F32)


def _dot_f32(a, b):
    return jnp.dot(a, b, preferred_element_type=F32, precision=lax.Precision.HIGHEST)


def _split_bf16(a):
    hi = a.astype(BF16)
    lo = (a - hi.astype(F32)).astype(BF16)
    return hi, lo


def _dot_3pass(a, b):
    a_hi, a_lo = _split_bf16(a)
    b_hi, b_lo = _split_bf16(b)
    return _dot(a_hi, b_hi) + (_dot(a_hi, b_lo) + _dot(a_lo, b_hi))


def _dot_exact_lhs(a_exact, b):
    b_hi, b_lo = _split_bf16(b)
    return _dot(a_exact, b_hi) + _dot(a_exact, b_lo)


def _sigmoid(x):
    return 1.0 / (1.0 + jnp.exp(-x))


def _layer_norm(y, g, b):
    mu = jnp.mean(y, axis=-1, keepdims=True)
    d = y - mu
    var = jnp.mean(d * d, axis=-1, keepdims=True)
    return d * lax.rsqrt(var + LN_EPS) * g + b


def _mm_kernel(a_ref, w_ref, o_ref):
    o_ref[...] = _dot(a_ref[...], w_ref[...]).astype(o_ref.dtype)


def _matmul(a, w_stack, layer, out_dtype, tm, tn):
    m, k = a.shape
    n = w_stack.shape[2]
    return pl.pallas_call(
        _mm_kernel,
        name="mm",
        grid=(m // tm, n // tn),
        in_specs=[pl.BlockSpec((tm, k), lambda i, j: (i, 0)),
                  pl.BlockSpec((None, k, tn), lambda i, j: (layer, 0, j))],
        out_specs=pl.BlockSpec((tm, tn), lambda i, j: (i, j)),
        out_shape=jax.ShapeDtypeStruct((m, n), out_dtype),
        compiler_params=_cparams(("parallel", "arbitrary")),
    )(a, w_stack)


GLA_TB = 256


def _decay_kernel(al_ref, wup_ref, ba_ref, cum_ref):
    c, tb = GLA_CHUNK, GLA_TB
    row = lax.broadcasted_iota(jnp.int32, (tb, tb), 0)
    col = lax.broadcasted_iota(jnp.int32, (tb, tb), 1)
    shift = c.bit_length() - 1
    same_chunk = lax.shift_right_logical(row, shift) == lax.shift_right_logical(col, shift)
    tril_blocks = (same_chunk & (row >= col)).astype(BF16)
    x = _dot_3pass(al_ref[...], wup_ref[...]) + ba_ref[...]
    log_a = (jnp.minimum(x, 0.0) - jnp.log1p(jnp.exp(-jnp.abs(x)))) / GLA_TAU
    cum_ref[...] = _dot_exact_lhs(tril_blocks, log_a)


def _decay(a_low, wup, ba, layer):
    n = a_low.shape[0]
    tb = GLA_TB
    width = GLA_HEADS * GLA_DK
    return pl.pallas_call(
        _decay_kernel,
        name="decay",
        grid=(n // tb,),
        in_specs=[pl.BlockSpec((tb, LANES), lambda i: (i, 0)),
                  pl.BlockSpec((None, LANES, width), lambda i: (layer, 0, 0)),
                  pl.BlockSpec((None, 1, width), lambda i: (layer, 0, 0))],
        out_specs=pl.BlockSpec((tb, width), lambda i: (i, 0)),
        out_shape=jax.ShapeDtypeStruct((n, width), F32),
        compiler_params=_cparams(("parallel",)),
    )(a_low, wup, ba)


def _rows_of(vectors, c):
    return jnp.concatenate([jnp.broadcast_to(v, (c, v.shape[-1])) for v in vectors], axis=0)


def _gla_kernel(q_ref, k_ref, v_ref, r_ref, cum_ref, g_ref, o_ref, st_ref):
    c, tb = GLA_CHUNK, GLA_TB
    nc = tb // c

    @pl.when(pl.program_id(2) == 0)
    def _():
        st_ref[...] = jnp.zeros_like(st_ref)

    cum = cum_ref[...]
    last = [cum[(ci + 1) * c - 1:(ci + 1) * c, :] for ci in range(nc)]
    before = [jnp.zeros_like(last[0])]
    for ci in range(nc):
        before.append(before[ci] + last[ci])
    total = before[nc]

    q = q_ref[...].astype(F32) * (GLA_DK ** -0.5)
    k = k_ref[...].astype(F32)
    q_in = q * jnp.exp(cum)
    k_in = (k * jnp.exp(-cum)).astype(BF16)
    k_st = k * jnp.exp(_rows_of(last, c) - cum)
    q_blk = (q_in * _rows_of([jnp.exp(b) for b in before[:nc]], c)).astype(BF16)
    k_blk = (k_st * _rows_of([jnp.exp(total - before[ci + 1]) for ci in range(nc)], c)
             ).astype(BF16)
    q_in = q_in.astype(BF16)
    k_st_b = k_st.astype(BF16)

    scores = []
    for ci in range(nc):
        keys = []
        for cj in range(nc):
            sj = slice(cj * c, (cj + 1) * c)
            if cj == ci:
                keys.append(k_in[sj])
            elif cj < ci - 1:
                keys.append((k_st[sj] * jnp.exp(before[ci] - before[cj + 1])).astype(BF16))
            else:
                keys.append(k_st_b[sj])
        scores.append(lax.dot_general(q_in[ci * c:(ci + 1) * c], jnp.concatenate(keys, axis=0),
                                      NT, preferred_element_type=F32))
    row = lax.broadcasted_iota(jnp.int32, (tb, tb), 0)
    col = lax.broadcasted_iota(jnp.int32, (tb, tb), 1)
    att = jnp.where(row >= col, jnp.concatenate(scores, axis=0), 0.0).astype(BF16)

    v = v_ref[...]
    st = st_ref[...]
    o = _dot(att, v) + lax.dot_general(q_blk, st.astype(BF16), NT, preferred_element_type=F32)
    st_ref[...] = jnp.exp(total) * st + lax.dot_general(v, k_blk, TN, preferred_element_type=F32)
    o = o * lax.rsqrt(jnp.mean(o * o, axis=-1, keepdims=True) + LN_EPS) * g_ref[...]
    r = r_ref[...].astype(F32)
    o_ref[...] = (o * (r * _sigmoid(r))).astype(o_ref.dtype)


def _gla(u, cum, g_norm, layer, batch, seq):
    n = u.shape[0]
    tb = GLA_TB
    nt = seq // tb
    rows = lambda b, h, t: b * nt + t
    kq = GLA_HEADS
    kv = 2 * GLA_HEADS * GLA_DK // GLA_DV
    kr = kv + GLA_HEADS
    return pl.pallas_call(
        _gla_kernel,
        name="gla",
        grid=(batch, GLA_HEADS, nt),
        in_specs=[
            pl.BlockSpec((tb, GLA_DK), lambda b, h, t: (rows(b, h, t), h)),
            pl.BlockSpec((tb, GLA_DK), lambda b, h, t: (rows(b, h, t), kq + h)),
            pl.BlockSpec((tb, GLA_DV), lambda b, h, t: (rows(b, h, t), kv + h)),
            pl.BlockSpec((tb, GLA_DV), lambda b, h, t: (rows(b, h, t), kr + h)),
            pl.BlockSpec((tb, GLA_DK), lambda b, h, t: (rows(b, h, t), h)),
            pl.BlockSpec((None, 1, GLA_DV), lambda b, h, t: (layer, 0, 0)),
        ],
        out_specs=pl.BlockSpec((tb, GLA_DV), lambda b, h, t: (rows(b, h, t), h)),
        out_shape=jax.ShapeDtypeStruct((n, GLA_HEADS * GLA_DV), BF16),
        scratch_shapes=[pltpu.VMEM((GLA_DV, GLA_DK), F32)],
        compiler_params=_cparams(("parallel", "parallel", "arbitrary")),
    )(u, u, u, u, cum, g_norm)


CONV_TT = 256
CONV_HALO = 32
CONV_RB = 128


def _conv_kernel(val_ref, gate_ref, w_ref, b_ref, lg_ref, lb_ref, o_ref, h_ref, c_ref):
    tt, halo, rb = CONV_TT, CONV_HALO, CONV_RB

    @pl.when(pl.program_id(1) == 0)
    def _():
        h_ref[pl.ds(0, halo), :] = jnp.zeros((halo, D_MODEL), F32)

    h_ref[pl.ds(halo, tt), :] = val_ref[...].astype(F32) * _sigmoid(gate_ref[...].astype(F32))

    def lane_block(li, carry):
        ls = pl.ds(pl.multiple_of(li * LANES, LANES), LANES)
        for r0 in range(0, tt, rb):
            win = h_ref[pl.ds(r0, rb + halo), ls]
            acc = jnp.broadcast_to(b_ref[:, ls], (rb, LANES))
            for b in range(SUBLANES):
                shifted = win if b == 0 else pltpu.roll(win, b, axis=0)
                for a in range(halo // SUBLANES):
                    d = SUBLANES * a + b
                    if d > CONV_WIDTH - 1:
                        continue
                    j = CONV_WIDTH - 1 - d
                    lo = halo - SUBLANES * a
                    acc = acc + w_ref[j:j + 1, ls] * shifted[lo:lo + rb, :]
            c_ref[pl.ds(r0, rb), ls] = acc
        return carry

    lax.fori_loop(0, D_MODEL // LANES, lane_block, 0)
    h_ref[pl.ds(0, halo), :] = h_ref[pl.ds(tt, halo), :]
    y = _layer_norm(c_ref[...], lg_ref[...], lb_ref[...])
    o_ref[...] = (y * _sigmoid(y)).astype(o_ref.dtype)


def _conv(u2, w_dw, b_dw, ln_g, ln_b, layer, batch, seq):
    n = u2.shape[0]
    tt = CONV_TT
    nt = seq // tt
    vec = pl.BlockSpec((None, 1, D_MODEL), lambda b, t: (layer, 0, 0))
    return pl.pallas_call(
        _conv_kernel,
        name="conv",
        grid=(batch, nt),
        in_specs=[
            pl.BlockSpec((tt, D_MODEL), lambda b, t: (b * nt + t, 0)),
            pl.BlockSpec((tt, D_MODEL), lambda b, t: (b * nt + t, 1)),
            pl.BlockSpec((None, CONV_WIDTH, D_MODEL), lambda b, t: (layer, 0, 0)),
            vec, vec, vec,
        ],
        out_specs=pl.BlockSpec((tt, D_MODEL), lambda b, t: (b * nt + t, 0)),
        out_shape=jax.ShapeDtypeStruct((n, D_MODEL), BF16),
        scratch_shapes=[pltpu.VMEM((tt + CONV_HALO, D_MODEL), F32),
                        pltpu.VMEM((tt, D_MODEL), F32)],
        compiler_params=_cparams(("parallel", "arbitrary")),
    )(u2, u2, w_dw, b_dw, ln_g, ln_b)


MERGE_TM = 256


def _merge_kernel(a1_ref, a2_ref, ga_ref, gb_ref, x_ref, wo_ref, wpw_ref, wout_ref,
                  lg_ref, lb_ref, xo_ref):
    ya = _dot(a1_ref[...], wo_ref[...])
    yb = _dot(a2_ref[...], wpw_ref[...])
    z = _sigmoid(ga_ref[...].astype(F32)) * ya + _sigmoid(gb_ref[...].astype(F32)) * yb
    mix = _dot(z.astype(BF16), wout_ref[...])
    xo_ref[...] = _layer_norm(DN_ALPHA * x_ref[...] + mix, lg_ref[...], lb_ref[...])


def _merge(a1, a2, u2, x, w_o, w_pw, w_out, ln_g, ln_b, layer):
    n = x.shape[0]
    tm = MERGE_TM
    row = lambda c: pl.BlockSpec((tm, D_MODEL), lambda i: (i, c))
    wfull = pl.BlockSpec((None, D_MODEL, D_MODEL), lambda i: (layer, 0, 0),
                         pipeline_mode=pl.Buffered(1))
    vec = pl.BlockSpec((None, 1, D_MODEL), lambda i: (2 * layer, 0, 0))
    return pl.pallas_call(
        _merge_kernel,
        name="merge",
        grid=(n // tm,),
        in_specs=[row(0), row(0), row(2), row(3), row(0), wfull, wfull, wfull, vec, vec],
        out_specs=row(0),
        out_shape=jax.ShapeDtypeStruct((n, D_MODEL), F32),
        compiler_params=_cparams(("parallel",)),
    )(a1, a2, u2, u2, x, w_o, w_pw, w_out, ln_g, ln_b)


FFN_TM = 512
FFN_TF = 512
FFN_SPLIT = 2


def _swiglu_accumulate(xb_ref, wg_ref, wu_ref, wd_ref, acc_ref, tm):
    rows = tm // FFN_SPLIT
    wg, wu, wd = wg_ref[...], wu_ref[...], wd_ref[...]
    for s in range(FFN_SPLIT):
        sl = pl.ds(s * rows, rows)
        xb = xb_ref[sl, :]
        g = _dot(xb, wg)
        u = _dot(xb, wu)
        h = (g * _sigmoid(g) * u).astype(BF16)
        acc_ref[sl, :] += _dot(h, wd)


def _ffn_kernel(wg_ref, wu_ref, wd_ref, x_ref, lg_ref, lb_ref, xo_ref, xbo_ref, xb_ref, acc_ref):
    f = pl.program_id(1)

    @pl.when(f == 0)
    def _():
        xb_ref[...] = x_ref[...].astype(BF16)
        acc_ref[...] = jnp.zeros_like(acc_ref)

    _swiglu_accumulate(xb_ref, wg_ref, wu_ref, wd_ref, acc_ref, FFN_TM)

    @pl.when(f == pl.num_programs(1) - 1)
    def _():
        y = _layer_norm(DN_ALPHA * x_ref[...] + acc_ref[...], lg_ref[...], lb_ref[...])
        xo_ref[...] = y
        xbo_ref[...] = y.astype(BF16)


def _ffn(x, wg, wu, wd, ln_g, ln_b, layer):
    n = x.shape[0]
    tm, tf = FFN_TM, FFN_TF
    il = layer // 2
    row = pl.BlockSpec((tm, D_MODEL), lambda i, f: (i, 0))
    vec = pl.BlockSpec((None, 1, D_MODEL), lambda i, f: (2 * layer + 1, 0, 0))
    return pl.pallas_call(
        _ffn_kernel,
        name="ffn",
        grid=(n // tm, FFN_DIM // tf),
        in_specs=[pl.BlockSpec((None, D_MODEL, tf), lambda i, f: (il, 0, f)),
                  pl.BlockSpec((None, D_MODEL, tf), lambda i, f: (il, 0, f)),
                  pl.BlockSpec((None, tf, D_MODEL), lambda i, f: (il, f, 0)),
                  row, vec, vec],
        out_specs=[row, row],
        out_shape=[jax.ShapeDtypeStruct((n, D_MODEL), F32),
                   jax.ShapeDtypeStruct((n, D_MODEL), BF16)],
        scratch_shapes=[pltpu.VMEM((tm, D_MODEL), BF16),
                        pltpu.VMEM((tm, D_MODEL), F32)],
        compiler_params=_cparams(("parallel", "arbitrary")),
    )(wg, wu, wd, x, ln_g, ln_b)


ROUTER_TM = 1024


def _router_kernel(x_ref, w_ref, idx_ref, wt_ref):
    logits = _dot_f32(x_ref[...], w_ref[...])
    lane = lax.broadcasted_iota(jnp.int32, logits.shape, 1)
    neg = jnp.float32(-jnp.inf)
    lg = jnp.where(lane < N_EXPERTS, logits, neg)
    m1 = jnp.max(lg, axis=-1, keepdims=True)
    i1 = jnp.min(jnp.where(lg == m1, lane, LANES), axis=-1, keepdims=True)
    lg2 = jnp.where(lane == i1, neg, lg)
    m2 = jnp.max(lg2, axis=-1, keepdims=True)
    i2 = jnp.min(jnp.where(lg2 == m2, lane, LANES), axis=-1, keepdims=True)
    e2 = jnp.exp(m2 - m1)
    w1 = 1.0 / (1.0 + e2)
    w2 = e2 / (1.0 + e2)
    idx_ref[...] = jnp.where(lane == 0, i1, jnp.where(lane == 1, i2, 0))
    wt_ref[...] = jnp.where(lane == 0, w1, jnp.where(lane == 1, w2, 0.0))


def _router(x, w_router_padded, layer):
    n = x.shape[0]
    tm = ROUTER_TM
    out = pl.BlockSpec((tm, LANES), lambda i: (i, 0))
    return pl.pallas_call(
        _router_kernel,
        name="router",
        grid=(n // tm,),
        in_specs=[pl.BlockSpec((tm, D_MODEL), lambda i: (i, 0)),
                  pl.BlockSpec((None, D_MODEL, LANES), lambda i: (layer // 2, 0, 0))],
        out_specs=[out, out],
        out_shape=[jax.ShapeDtypeStruct((n, LANES), jnp.int32),
                   jax.ShapeDtypeStruct((n, LANES), F32)],
        compiler_params=_cparams(("parallel",)),
    )(x, w_router_padded)


MOE_TM = 512
MOE_TF = 512
GATHER_UNROLL = 8


def _row_copy(src_hbm, dst_vmem, sem, src_row, dst_row):
    return pltpu.make_async_copy(src_hbm.at[pl.ds(src_row, 1), :],
                                 dst_vmem.at[pl.ds(dst_row, 1), :], sem)


def _gather_rows(src_hbm, idx_ref, dst_vmem, sem, n_rows):
    def issue(r, carry):
        _row_copy(src_hbm, dst_vmem, sem, idx_ref[0, r], r).start()
        return carry
    lax.fori_loop(0, n_rows, issue, 0, unroll=GATHER_UNROLL)


def _wait_rows(src_hbm, dst_vmem, sem, n_rows):
    def drain(r, carry):
        _row_copy(src_hbm, dst_vmem, sem, 0, r).wait()
        return carry
    lax.fori_loop(0, n_rows, drain, 0, unroll=GATHER_UNROLL)


def _moe_kernel(be_ref, nu_ref, tok_ref, tokn_ref, x_hbm, wg_ref, wu_ref, wd_ref, y_ref,
                xf_ref, xb_ref, acc_ref, sem):
    i = pl.program_id(0)
    f = pl.program_id(1)
    last = pl.num_programs(1) - 1
    n_used = nu_ref[0]
    used = i < n_used
    slot = lax.rem(i, 2)

    @pl.when(used & (f == 0))
    def _():
        @pl.when(i == 0)
        def _():
            _gather_rows(x_hbm, tok_ref, xf_ref.at[0], sem.at[0], MOE_TM)

        _wait_rows(x_hbm, xf_ref.at[slot], sem.at[slot], MOE_TM)
        xb_ref[...] = xf_ref[slot].astype(BF16)
        acc_ref[...] = jnp.zeros_like(acc_ref)

        @pl.when(i + 1 < n_used)
        def _():
            _gather_rows(x_hbm, tokn_ref, xf_ref.at[1 - slot], sem.at[1 - slot], MOE_TM)

    @pl.when(used)
    def _():
        _swiglu_accumulate(xb_ref, wg_ref, wu_ref, wd_ref, acc_ref, MOE_TM)

    @pl.when(f == last)
    def _():
        @pl.when(used)
        def _():
            y_ref[...] = acc_ref[...]

        @pl.when(jnp.logical_not(used))
        def _():
            y_ref[...] = jnp.zeros_like(y_ref)


def _moe_ffn(block_e, n_used, row_tok, x, wg, wu, wd, layer):
    n_rows = row_tok.shape[0]
    tm, tf = MOE_TM, MOE_TF
    nb = n_rows // tm
    nf = FFN_DIM // tf
    il = layer // 2
    tok3 = row_tok.reshape(nb, 1, tm)

    def fsel(i, f, nu):
        return jnp.where(i < nu[0], f, nf - 1)

    grid_spec = pltpu.PrefetchScalarGridSpec(
        num_scalar_prefetch=2,
        grid=(nb, nf),
        in_specs=[
            pl.BlockSpec((None, 1, tm), lambda i, f, be, nu: (i, 0, 0), memory_space=pltpu.SMEM),
            pl.BlockSpec((None, 1, tm), lambda i, f, be, nu: (jnp.minimum(i + 1, nb - 1), 0, 0),
                         memory_space=pltpu.SMEM),
            pl.BlockSpec(memory_space=pl.ANY),
            pl.BlockSpec((None, None, D_MODEL, tf),
                         lambda i, f, be, nu: (il, be[i], 0, fsel(i, f, nu))),
            pl.BlockSpec((None, None, D_MODEL, tf),
                         lambda i, f, be, nu: (il, be[i], 0, fsel(i, f, nu))),
            pl.BlockSpec((None, None, tf, D_MODEL),
                         lambda i, f, be, nu: (il, be[i], fsel(i, f, nu), 0)),
        ],
        out_specs=pl.BlockSpec((tm, D_MODEL), lambda i, f, be, nu: (i, 0)),
        scratch_shapes=[pltpu.VMEM((2, tm, D_MODEL), F32),
                        pltpu.VMEM((tm, D_MODEL), BF16),
                        pltpu.VMEM((tm, D_MODEL), F32),
                        pltpu.SemaphoreType.DMA((2,))],
    )
    return pl.pallas_call(
        _moe_kernel,
        name="moe_ffn",
        grid_spec=grid_spec,
        out_shape=jax.ShapeDtypeStruct((n_rows, D_MODEL), F32),
        compiler_params=_cparams(("arbitrary", "arbitrary")),
    )(block_e, n_used, tok3, tok3, x, wg, wu, wd)


COMB_TT = 256


def _combine_kernel(pos0_ref, pos1_ref, y_hbm, wt_ref, x_ref, lg_ref, lb_ref, xo_ref, xbo_ref,
                    g0_ref, g1_ref, sem):
    tt = COMB_TT
    _gather_rows(y_hbm, pos0_ref, g0_ref, sem.at[0], tt)
    _gather_rows(y_hbm, pos1_ref, g1_ref, sem.at[1], tt)
    _wait_rows(y_hbm, g0_ref, sem.at[0], tt)
    _wait_rows(y_hbm, g1_ref, sem.at[1], tt)
    wt = wt_ref[...]
    f = wt[:, 0:1] * g0_ref[...] + wt[:, 1:2] * g1_ref[...]
    y = _layer_norm(DN_ALPHA * x_ref[...] + f, lg_ref[...], lb_ref[...])
    xo_ref[...] = y
    xbo_ref[...] = y.astype(BF16)


def _combine(pos, yrows, wt, x, ln_g, ln_b, layer):
    n = x.shape[0]
    tt = COMB_TT
    pos0 = pos[0::TOP_K].reshape(n // tt, 1, tt)
    pos1 = pos[1::TOP_K].reshape(n // tt, 1, tt)
    row = pl.BlockSpec((tt, D_MODEL), lambda i: (i, 0))
    vec = pl.BlockSpec((None, 1, D_MODEL), lambda i: (2 * layer + 1, 0, 0))
    idx = pl.BlockSpec((None, 1, tt), lambda i: (i, 0, 0), memory_space=pltpu.SMEM)
    return pl.pallas_call(
        _combine_kernel,
        name="combine",
        grid=(n // tt,),
        in_specs=[idx, idx,
                  pl.BlockSpec(memory_space=pl.ANY),
                  pl.BlockSpec((tt, LANES), lambda i: (i, 0)),
                  row, vec, vec],
        out_specs=[row, row],
        out_shape=[jax.ShapeDtypeStruct((n, D_MODEL), F32),
                   jax.ShapeDtypeStruct((n, D_MODEL), BF16)],
        scratch_shapes=[pltpu.VMEM((tt, D_MODEL), F32),
                        pltpu.VMEM((tt, D_MODEL), F32),
                        pltpu.SemaphoreType.DMA((2,))],
        compiler_params=_cparams(("arbitrary",)),
    )(pos0, pos1, yrows, wt, x, ln_g, ln_b)


def _routing_plan(idx, n):
    n_assign = n * TOP_K
    e_flat = idx[:, :TOP_K].reshape(-1)
    onehot = (e_flat[:, None] == jnp.arange(N_EXPERTS, dtype=jnp.int32)[None, :]).astype(jnp.int32)
    csum = jnp.cumsum(onehot, axis=0)
    rank = jnp.sum(csum * onehot, axis=1) - 1
    counts = csum[-1]
    padded = (counts + MOE_TM - 1) // MOE_TM * MOE_TM
    pend = jnp.cumsum(padded)
    pstart = pend - padded
    pos = (jnp.sum(pstart[None, :] * onehot, axis=1) + rank).astype(jnp.int32)
    n_rows = (-(-n_assign // MOE_TM) + N_EXPERTS) * MOE_TM
    nb = n_rows // MOE_TM
    row_tok = jnp.zeros((n_rows,), jnp.int32).at[pos].set(
        jnp.arange(n_assign, dtype=jnp.int32) // TOP_K)
    block_start = jnp.arange(nb, dtype=jnp.int32) * MOE_TM
    block_e = jnp.minimum(jnp.sum((pend[None, :] <= block_start[:, None]).astype(jnp.int32), axis=1),
                          N_EXPERTS - 1).astype(jnp.int32)
    n_used = (pend[-1:] // MOE_TM).astype(jnp.int32)
    return pos, row_tok, block_e, n_used


def kernel(x, w_in, gla_a_up, gla_a_bias, gla_norm_g, gla_w_o, conv_w_dw, conv_b_dw, conv_ln_g, conv_ln_b, conv_w_pw, w_out, ln_g, ln_b, ffn_w_gate, ffn_w_up, ffn_w_down, moe_w_router, moe_w_gate, moe_w_up, moe_w_down):
    batch, seq, d = x.shape
    n = batch * seq
    xf = x.reshape(n, d)
    xb = xf.astype(BF16)
    a_lo, a_hi = QKVR, QKVR + GLA_RANK
    w_qkvr = w_in[:, :, :a_lo].astype(BF16)
    w_rest = w_in[:, :, a_hi:].astype(BF16)
    w_alow = jnp.pad(w_in[:, :, a_lo:a_hi], ((0, 0), (0, 0), (0, LANES - GLA_RANK))).astype(BF16)
    wup = jnp.pad(gla_a_up, ((0, 0), (0, LANES - GLA_RANK), (0, 0)))
    w_o, w_pw, w_mix = gla_w_o.astype(BF16), conv_w_pw.astype(BF16), w_out.astype(BF16)
    ffn_g, ffn_u, ffn_d = ffn_w_gate.astype(BF16), ffn_w_up.astype(BF16), ffn_w_down.astype(BF16)
    moe_g, moe_u, moe_d = moe_w_gate.astype(BF16), moe_w_up.astype(BF16), moe_w_down.astype(BF16)
    w_r = jnp.pad(moe_w_router, ((0, 0), (0, 0), (0, LANES - N_EXPERTS)))
    ba, gn = gla_a_bias[:, None, :], gla_norm_g[:, None, :]
    cb, cg, cbeta = conv_b_dw[:, None, :], conv_ln_g[:, None, :], conv_ln_b[:, None, :]
    lng = ln_g.reshape(2 * DEPTH, 1, d)
    lnb = ln_b.reshape(2 * DEPTH, 1, d)
    for layer in range(DEPTH):
        u1 = _matmul(xb, w_qkvr, layer, BF16, 1024, 1024)
        u2 = _matmul(xb, w_rest, layer, BF16, 1024, 1024)
        a_low = _matmul(xb, w_alow, layer, F32, 1024, LANES)
        cum = _decay(a_low, wup, ba, layer)
        a1 = _gla(u1, cum, gn, layer, batch, seq)
        a2 = _conv(u2, conv_w_dw, cb, cg, cbeta, layer, batch, seq)
        xf = _merge(a1, a2, u2, xf, w_o, w_pw, w_mix, lng, lnb, layer)
        if layer % 2 == 0:
            xf, xb = _ffn(xf, ffn_g, ffn_u, ffn_d, lng, lnb, layer)
        else:
            idx, wt = _router(xf, w_r, layer)
            pos, row_tok, block_e, n_used = _routing_plan(idx, n)
            yrows = _moe_ffn(block_e, n_used, row_tok, xf, moe_g, moe_u, moe_d, layer)
            xf, xb = _combine(pos, yrows, wt, xf, lng, lnb, layer)
    return xf.reshape(batch, seq, d)
```

```python
import jax
import jax.numpy as jnp
from jax import lax
from jax.experimental import pallas as pl
from jax.experimental.pallas import tpu as pltpu

D_MODEL = 2048
DEPTH = 4
GLA_HEADS = 4
GLA_DK = 256
GLA_DV = 512
GLA_RANK = 16
GLA_TAU = 16.0
GLA_CHUNK = 64
CONV_WIDTH = 31
FFN_DIM = 5632
N_EXPERTS = 8
TOP_K = 2
DN_ALPHA = (2 * DEPTH) ** 0.25
LN_EPS = 1e-5

LANES = 128
SUBLANES = 8
QKVR = 2 * GLA_HEADS * GLA_DK + 2 * GLA_HEADS * GLA_DV
VMEM_LIMIT = 56 * 1024 * 1024

F32 = jnp.float32
BF16 = jnp.bfloat16
NT = (((1,), (1,)), ((), ()))
TN = (((0,), (0,)), ((), ()))


def _cparams(sem):
    return pltpu.CompilerParams(dimension_semantics=sem, vmem_limit_bytes=VMEM_LIMIT)


def _dot(a, b):
    return jnp.dot(a, b, preferred_element_type=F32)


def _split_bf16(a):
    hi = a.astype(BF16)
    lo = (a - hi.astype(F32)).astype(BF16)
    return hi, lo


def _dot_3pass(a, b):
    a_hi, a_lo = _split_bf16(a)
    b_hi, b_lo = _split_bf16(b)
    return _dot(a_hi, b_hi) + (_dot(a_hi, b_lo) + _dot(a_lo, b_hi))


def _dot_exact_lhs(a_exact, b):
    b_hi, b_lo = _split_bf16(b)
    return _dot(a_exact, b_hi) + _dot(a_exact, b_lo)


def _sigmoid(x):
    return 1.0 / (1.0 + jnp.exp(-x))


def _layer_norm(y, g, b):
    mu = jnp.mean(y, axis=-1, keepdims=True)
    d = y - mu
    var = jnp.mean(d * d, axis=-1, keepdims=True)
    return d * lax.rsqrt(var + LN_EPS) * g + b


def _mm_kernel(a_ref, w_ref, o_ref):
    o_ref[...] = _dot(a_ref[...], w_ref[...]).astype(o_ref.dtype)


def _mm_cast_kernel(a_ref, w_ref, side_ref, o_ref, side_out_ref):
    o_ref[...] = _dot(a_ref[...], w_ref[...]).astype(o_ref.dtype)
    side_out_ref[...] = side_ref[...].astype(side_out_ref.dtype)


def _side_spec(side, n_steps, step_of):
    rows, cols = side.shape
    assert rows % n_steps == 0 and (rows // n_steps) % (2 * SUBLANES) == 0
    return pl.BlockSpec((rows // n_steps, cols), lambda *g: (step_of(*g), 0))


def _matmul(a, w_stack, layer, out_dtype, tm, tn, n=None, side=None):
    m, k = a.shape
    n = w_stack.shape[2] if n is None else n
    nj = n // tn
    in_specs = [pl.BlockSpec((tm, k), lambda i, j: (i, 0)),
                pl.BlockSpec((None, k, tn), lambda i, j: (layer, 0, j))]
    out_spec = pl.BlockSpec((tm, tn), lambda i, j: (i, j))
    out_shape = jax.ShapeDtypeStruct((m, n), out_dtype)
    if side is None:
        return pl.pallas_call(
            _mm_kernel, name="mm", grid=(m // tm, nj), in_specs=in_specs, out_specs=out_spec,
            out_shape=out_shape, compiler_params=_cparams(("parallel", "arbitrary")),
        )(a, w_stack)
    sspec = _side_spec(side, (m // tm) * nj, lambda i, j: i * nj + j)
    return pl.pallas_call(
        _mm_cast_kernel, name="mm_cast", grid=(m // tm, nj),
        in_specs=in_specs + [sspec], out_specs=[out_spec, sspec],
        out_shape=[out_shape, jax.ShapeDtypeStruct(side.shape, BF16)],
        compiler_params=_cparams(("arbitrary", "arbitrary")),
    )(a, w_stack, side)


GLA_TB = 256


def _decay_kernel(al_ref, wup_ref, ba_ref, cum_ref):
    c, tb = GLA_CHUNK, GLA_TB
    row = lax.broadcasted_iota(jnp.int32, (tb, tb), 0)
    col = lax.broadcasted_iota(jnp.int32, (tb, tb), 1)
    shift = c.bit_length() - 1
    same_chunk = lax.shift_right_logical(row, shift) == lax.shift_right_logical(col, shift)
    tril_blocks = (same_chunk & (row >= col)).astype(BF16)
    x = _dot_3pass(al_ref[...], wup_ref[...]) + ba_ref[...]
    log_a = (jnp.minimum(x, 0.0) - jnp.log1p(jnp.exp(-jnp.abs(x)))) / GLA_TAU
    cum_ref[...] = _dot_exact_lhs(tril_blocks, log_a)


def _decay(a_low, wup, ba, layer):
    n = a_low.shape[0]
    tb = GLA_TB
    width = GLA_HEADS * GLA_DK
    return pl.pallas_call(
        _decay_kernel,
        name="decay",
        grid=(n // tb,),
        in_specs=[pl.BlockSpec((tb, LANES), lambda i: (i, 0)),
                  pl.BlockSpec((None, LANES, width), lambda i: (layer, 0, 0)),
                  pl.BlockSpec((None, 1, width), lambda i: (layer, 0, 0))],
        out_specs=pl.BlockSpec((tb, width), lambda i: (i, 0)),
        out_shape=jax.ShapeDtypeStruct((n, width), F32),
        compiler_params=_cparams(("parallel",)),
    )(a_low, wup, ba)


def _rows_of(vectors, c):
    return jnp.concatenate([jnp.broadcast_to(v, (c, v.shape[-1])) for v in vectors], axis=0)


def _gla_kernel(q_ref, k_ref, v_ref, r_ref, cum_ref, g_ref, o_ref, st_ref):
    c, tb = GLA_CHUNK, GLA_TB
    nc = tb // c

    @pl.when(pl.program_id(2) == 0)
    def _():
        st_ref[...] = jnp.zeros_like(st_ref)

    cum = cum_ref[...]
    last = [cum[(ci + 1) * c - 1:(ci + 1) * c, :] for ci in range(nc)]
    before = [jnp.zeros_like(last[0])]
    for ci in range(nc):
        before.append(before[ci] + last[ci])
    total = before[nc]

    q = q_ref[...].astype(F32) * (GLA_DK ** -0.5)
    k = k_ref[...].astype(F32)
    q_in = q * jnp.exp(cum)
    k_in = (k * jnp.exp(-cum)).astype(BF16)
    k_st = k * jnp.exp(_rows_of(last, c) - cum)
    q_blk = (q_in * _rows_of([jnp.exp(b) for b in before[:nc]], c)).astype(BF16)
    k_blk = (k_st * _rows_of([jnp.exp(total - before[ci + 1]) for ci in range(nc)], c)
             ).astype(BF16)
    q_in = q_in.astype(BF16)
    k_st_b = k_st.astype(BF16)

    scores = []
    for ci in range(nc):
        keys = []
        for cj in range(nc):
            sj = slice(cj * c, (cj + 1) * c)
            if cj == ci:
                keys.append(k_in[sj])
            elif cj < ci - 1:
                keys.append((k_st[sj] * jnp.exp(before[ci] - before[cj + 1])).astype(BF16))
            else:
                keys.append(k_st_b[sj])
        scores.append(lax.dot_general(q_in[ci * c:(ci + 1) * c], jnp.concatenate(keys, axis=0),
                                      NT, preferred_element_type=F32))
    row = lax.broadcasted_iota(jnp.int32, (tb, tb), 0)
    col = lax.broadcasted_iota(jnp.int32, (tb, tb), 1)
    att = jnp.where(row >= col, jnp.concatenate(scores, axis=0), 0.0).astype(BF16)

    v = v_ref[...]
    st = st_ref[...]
    o = _dot(att, v) + lax.dot_general(q_blk, st.astype(BF16), NT, preferred_element_type=F32)
    st_ref[...] = jnp.exp(total) * st + lax.dot_general(v, k_blk, TN, preferred_element_type=F32)
    o = o * lax.rsqrt(jnp.mean(o * o, axis=-1, keepdims=True) + LN_EPS) * g_ref[...]
    r = r_ref[...].astype(F32)
    o_ref[...] = (o * (r * _sigmoid(r))).astype(o_ref.dtype)


def _gla(u, cum, g_norm, layer, batch, seq):
    n = u.shape[0]
    tb = GLA_TB
    nt = seq // tb
    rows = lambda b, h, t: b * nt + t
    kq = GLA_HEADS
    kv = 2 * GLA_HEADS * GLA_DK // GLA_DV
    kr = kv + GLA_HEADS
    return pl.pallas_call(
        _gla_kernel,
        name="gla",
        grid=(batch, GLA_HEADS, nt),
        in_specs=[
            pl.BlockSpec((tb, GLA_DK), lambda b, h, t: (rows(b, h, t), h)),
            pl.BlockSpec((tb, GLA_DK), lambda b, h, t: (rows(b, h, t), kq + h)),
            pl.BlockSpec((tb, GLA_DV), lambda b, h, t: (rows(b, h, t), kv + h)),
            pl.BlockSpec((tb, GLA_DV), lambda b, h, t: (rows(b, h, t), kr + h)),
            pl.BlockSpec((tb, GLA_DK), lambda b, h, t: (rows(b, h, t), h)),
            pl.BlockSpec((None, 1, GLA_DV), lambda b, h, t: (layer, 0, 0)),
        ],
        out_specs=pl.BlockSpec((tb, GLA_DV), lambda b, h, t: (rows(b, h, t), h)),
        out_shape=jax.ShapeDtypeStruct((n, GLA_HEADS * GLA_DV), BF16),
        scratch_shapes=[pltpu.VMEM((GLA_DV, GLA_DK), F32)],
        compiler_params=_cparams(("parallel", "parallel", "arbitrary")),
    )(u, u, u, u, cum, g_norm)


CONV_TT = 256
CONV_HALO = 32
CONV_RB = 128


def _conv_kernel(val_ref, gate_ref, w_ref, b_ref, lg_ref, lb_ref, o_ref, h_ref, c_ref):
    tt, halo, rb = CONV_TT, CONV_HALO, CONV_RB

    @pl.when(pl.program_id(1) == 0)
    def _():
        h_ref[pl.ds(0, halo), :] = jnp.zeros((halo, D_MODEL), F32)

    h_ref[pl.ds(halo, tt), :] = val_ref[...].astype(F32) * _sigmoid(gate_ref[...].astype(F32))

    def lane_block(li, carry):
        ls = pl.ds(pl.multiple_of(li * LANES, LANES), LANES)
        for r0 in range(0, tt, rb):
            win = h_ref[pl.ds(r0, rb + halo), ls]
            acc = jnp.broadcast_to(b_ref[:, ls], (rb, LANES))
            for b in range(SUBLANES):
                shifted = win if b == 0 else pltpu.roll(win, b, axis=0)
                for a in range(halo // SUBLANES):
                    d = SUBLANES * a + b
                    if d > CONV_WIDTH - 1:
                        continue
                    j = CONV_WIDTH - 1 - d
                    lo = halo - SUBLANES * a
                    acc = acc + w_ref[j:j + 1, ls] * shifted[lo:lo + rb, :]
            c_ref[pl.ds(r0, rb), ls] = acc
        return carry

    lax.fori_loop(0, D_MODEL // LANES, lane_block, 0)
    h_ref[pl.ds(0, halo), :] = h_ref[pl.ds(tt, halo), :]
    y = _layer_norm(c_ref[...], lg_ref[...], lb_ref[...])
    o_ref[...] = (y * _sigmoid(y)).astype(o_ref.dtype)


def _conv(u2, w_dw, b_dw, ln_g, ln_b, layer, batch, seq):
    n = u2.shape[0]
    tt = CONV_TT
    nt = seq // tt
    vec = pl.BlockSpec((None, 1, D_MODEL), lambda b, t: (layer, 0, 0))
    return pl.pallas_call(
        _conv_kernel,
        name="conv",
        grid=(batch, nt),
        in_specs=[
            pl.BlockSpec((tt, D_MODEL), lambda b, t: (b * nt + t, 0)),
            pl.BlockSpec((tt, D_MODEL), lambda b, t: (b * nt + t, 1)),
            pl.BlockSpec((None, CONV_WIDTH, D_MODEL), lambda b, t: (layer, 0, 0)),
            vec, vec, vec,
        ],
        out_specs=pl.BlockSpec((tt, D_MODEL), lambda b, t: (b * nt + t, 0)),
        out_shape=jax.ShapeDtypeStruct((n, D_MODEL), BF16),
        scratch_shapes=[pltpu.VMEM((tt + CONV_HALO, D_MODEL), F32),
                        pltpu.VMEM((tt, D_MODEL), F32)],
        compiler_params=_cparams(("parallel", "arbitrary")),
    )(u2, u2, w_dw, b_dw, ln_g, ln_b)


MERGE_TM = 256


def _merge_kernel(a1_ref, a2_ref, ga_ref, gb_ref, x_ref, wo_ref, wpw_ref, wout_ref,
                  lg_ref, lb_ref, xo_ref):
    ya = _dot(a1_ref[...], wo_ref[...])
    yb = _dot(a2_ref[...], wpw_ref[...])
    z = _sigmoid(ga_ref[...].astype(F32)) * ya + _sigmoid(gb_ref[...].astype(F32)) * yb
    mix = _dot(z.astype(BF16), wout_ref[...])
    xo_ref[...] = _layer_norm(DN_ALPHA * x_ref[...] + mix, lg_ref[...], lb_ref[...])


def _merge(a1, a2, u2, x, w_o, w_pw, w_out, ln_g, ln_b, layer):
    n = x.shape[0]
    tm = MERGE_TM
    row = lambda c: pl.BlockSpec((tm, D_MODEL), lambda i: (i, c))
    wfull = pl.BlockSpec((None, D_MODEL, D_MODEL), lambda i: (layer, 0, 0),
                         pipeline_mode=pl.Buffered(1))
    vec = pl.BlockSpec((None, 1, D_MODEL), lambda i: (2 * layer, 0, 0))
    return pl.pallas_call(
        _merge_kernel,
        name="merge",
        grid=(n // tm,),
        in_specs=[row(0), row(0), row(2), row(3), row(0), wfull, wfull, wfull, vec, vec],
        out_specs=row(0),
        out_shape=jax.ShapeDtypeStruct((n, D_MODEL), F32),
        compiler_params=_cparams(("parallel",)),
    )(a1, a2, u2, u2, x, w_o, w_pw, w_out, ln_g, ln_b)


FFN_TM = 512
FFN_TF = 512
FFN_SPLIT = 2


def _swiglu_accumulate(xb_ref, wg_ref, wu_ref, wd_ref, acc_ref, tm):
    rows = tm // FFN_SPLIT
    wg, wu, wd = wg_ref[...], wu_ref[...], wd_ref[...]
    for s in range(FFN_SPLIT):
        sl = pl.ds(s * rows, rows)
        xb = xb_ref[sl, :]
        g = _dot(xb, wg)
        u = _dot(xb, wu)
        h = (g * _sigmoid(g) * u).astype(BF16)
        acc_ref[sl, :] += _dot(h, wd)


def _ffn_cast_kernel(wg_ref, wu_ref, wd_ref, x_ref, lg_ref, lb_ref, side_ref,
                     xo_ref, xbo_ref, side_out_ref, xb_ref, acc_ref):
    side_out_ref[...] = side_ref[...].astype(side_out_ref.dtype)
    _ffn_kernel(wg_ref, wu_ref, wd_ref, x_ref, lg_ref, lb_ref, xo_ref, xbo_ref, xb_ref, acc_ref)


def _ffn_kernel(wg_ref, wu_ref, wd_ref, x_ref, lg_ref, lb_ref, xo_ref, xbo_ref, xb_ref, acc_ref):
    f = pl.program_id(1)

    @pl.when(f == 0)
    def _():
        xb_ref[...] = x_ref[...].astype(BF16)
        acc_ref[...] = jnp.zeros_like(acc_ref)

    _swiglu_accumulate(xb_ref, wg_ref, wu_ref, wd_ref, acc_ref, FFN_TM)

    @pl.when(f == pl.num_programs(1) - 1)
    def _():
        y = _layer_norm(DN_ALPHA * x_ref[...] + acc_ref[...], lg_ref[...], lb_ref[...])
        xo_ref[...] = y
        xbo_ref[...] = y.astype(BF16)


def _ffn(x, wg, wu, wd, ln_g, ln_b, layer, side=None):
    n = x.shape[0]
    tm, tf = FFN_TM, FFN_TF
    il = layer // 2
    nf = FFN_DIM // tf
    row = pl.BlockSpec((tm, D_MODEL), lambda i, f: (i, 0))
    vec = pl.BlockSpec((None, 1, D_MODEL), lambda i, f: (2 * layer + 1, 0, 0))
    in_specs = [pl.BlockSpec((None, D_MODEL, tf), lambda i, f: (il, 0, f)),
                pl.BlockSpec((None, D_MODEL, tf), lambda i, f: (il, 0, f)),
                pl.BlockSpec((None, tf, D_MODEL), lambda i, f: (il, f, 0)),
                row, vec, vec]
    out_specs = [row, row]
    out_shape = [jax.ShapeDtypeStruct((n, D_MODEL), F32), jax.ShapeDtypeStruct((n, D_MODEL), BF16)]
    scratch = [pltpu.VMEM((tm, D_MODEL), BF16), pltpu.VMEM((tm, D_MODEL), F32)]
    if side is None:
        return pl.pallas_call(
            _ffn_kernel, name="ffn", grid=(n // tm, nf), in_specs=in_specs, out_specs=out_specs,
            out_shape=out_shape, scratch_shapes=scratch,
            compiler_params=_cparams(("parallel", "arbitrary")),
        )(wg, wu, wd, x, ln_g, ln_b)
    sspec = _side_spec(side, (n // tm) * nf, lambda i, f: i * nf + f)
    return pl.pallas_call(
        _ffn_cast_kernel, name="ffn_cast", grid=(n // tm, nf),
        in_specs=in_specs + [sspec], out_specs=out_specs + [sspec],
        out_shape=out_shape + [jax.ShapeDtypeStruct(side.shape, BF16)], scratch_shapes=scratch,
        compiler_params=_cparams(("arbitrary", "arbitrary")),
    )(wg, wu, wd, x, ln_g, ln_b, side)


ROUTER_TM = 1024


def _router_kernel(x_ref, w_ref, idx_ref, wt_ref):
    logits = _dot_3pass(x_ref[...], w_ref[...])
    lane = lax.broadcasted_iota(jnp.int32, logits.shape, 1)
    neg = jnp.float32(-jnp.inf)
    lg = jnp.where(lane < N_EXPERTS, logits, neg)
    m1 = jnp.max(lg, axis=-1, keepdims=True)
    i1 = jnp.min(jnp.where(lg == m1, lane, LANES), axis=-1, keepdims=True)
    lg2 = jnp.where(lane == i1, neg, lg)
    m2 = jnp.max(lg2, axis=-1, keepdims=True)
    i2 = jnp.min(jnp.where(lg2 == m2, lane, LANES), axis=-1, keepdims=True)
    e2 = jnp.exp(m2 - m1)
    w1 = 1.0 / (1.0 + e2)
    w2 = e2 / (1.0 + e2)
    idx_ref[...] = jnp.where(lane == 0, i1, jnp.where(lane == 1, i2, 0))
    wt_ref[...] = jnp.where(lane == 0, w1, jnp.where(lane == 1, w2, 0.0))


def _router(x, w_router_padded, layer):
    n = x.shape[0]
    tm = ROUTER_TM
    out = pl.BlockSpec((tm, LANES), lambda i: (i, 0))
    return pl.pallas_call(
        _router_kernel,
        name="router",
        grid=(n // tm,),
        in_specs=[pl.BlockSpec((tm, D_MODEL), lambda i: (i, 0)),
                  pl.BlockSpec((None, D_MODEL, LANES), lambda i: (layer // 2, 0, 0))],
        out_specs=[out, out],
        out_shape=[jax.ShapeDtypeStruct((n, LANES), jnp.int32),
                   jax.ShapeDtypeStruct((n, LANES), F32)],
        compiler_params=_cparams(("parallel",)),
    )(x, w_router_padded)


MOE_TM = 512
MOE_TF = 512
GATHER_UNROLL = 8


def _row_copy(src_hbm, dst_vmem, sem, src_row, dst_row):
    return pltpu.make_async_copy(src_hbm.at[pl.ds(src_row, 1), :],
                                 dst_vmem.at[pl.ds(dst_row, 1), :], sem)


def _gather_rows(src_hbm, idx_ref, dst_vmem, sem, n_rows):
    def issue(r, carry):
        _row_copy(src_hbm, dst_vmem, sem, idx_ref[0, r], r).start()
        return carry
    lax.fori_loop(0, n_rows, issue, 0, unroll=GATHER_UNROLL)


def _wait_rows(src_hbm, dst_vmem, sem, n_rows):
    def drain(r, carry):
        _row_copy(src_hbm, dst_vmem, sem, 0, r).wait()
        return carry
    lax.fori_loop(0, n_rows, drain, 0, unroll=GATHER_UNROLL)


def _moe_kernel(be_ref, nu_ref, tok_ref, tokn_ref, x_hbm, wg_ref, wu_ref, wd_ref, y_ref,
                xf_ref, xb_ref, sem):
    i = pl.program_id(0)
    f = pl.program_id(1)
    last = pl.num_programs(1) - 1
    n_used = nu_ref[0]
    used = i < n_used
    slot = lax.rem(i, 2)

    @pl.when(used & (f == 0))
    def _():
        @pl.when(i == 0)
        def _():
            _gather_rows(x_hbm, tok_ref, xf_ref.at[0], sem.at[0], MOE_TM)

        _wait_rows(x_hbm, xf_ref.at[slot], sem.at[slot], MOE_TM)
        xb_ref[...] = xf_ref[slot].astype(BF16)
        y_ref[...] = jnp.zeros_like(y_ref)

        @pl.when(i + 1 < n_used)
        def _():
            _gather_rows(x_hbm, tokn_ref, xf_ref.at[1 - slot], sem.at[1 - slot], MOE_TM)

    @pl.when(used)
    def _():
        _swiglu_accumulate(xb_ref, wg_ref, wu_ref, wd_ref, y_ref, MOE_TM)

    @pl.when(jnp.logical_not(used) & (f == last))
    def _():
        y_ref[...] = jnp.zeros_like(y_ref)


def _moe_ffn(block_e, n_used, row_tok, x, wg, wu, wd, layer):
    n_rows = row_tok.shape[0]
    tm, tf = MOE_TM, MOE_TF
    nb = n_rows // tm
    nf = FFN_DIM // tf
    il = layer // 2
    tok3 = row_tok.reshape(nb, 1, tm)

    def fsel(i, f, nu):
        return jnp.where(i < nu[0], f, nf - 1)

    grid_spec = pltpu.PrefetchScalarGridSpec(
        num_scalar_prefetch=2,
        grid=(nb, nf),
        in_specs=[
            pl.BlockSpec((None, 1, tm), lambda i, f, be, nu: (i, 0, 0), memory_space=pltpu.SMEM),
            pl.BlockSpec((None, 1, tm), lambda i, f, be, nu: (jnp.minimum(i + 1, nb - 1), 0, 0),
                         memory_space=pltpu.SMEM),
            pl.BlockSpec(memory_space=pl.ANY),
            pl.BlockSpec((None, None, D_MODEL, tf),
                         lambda i, f, be, nu: (il, be[i], 0, fsel(i, f, nu))),
            pl.BlockSpec((None, None, D_MODEL, tf),
                         lambda i, f, be, nu: (il, be[i], 0, fsel(i, f, nu))),
            pl.BlockSpec((None, None, tf, D_MODEL),
                         lambda i, f, be, nu: (il, be[i], fsel(i, f, nu), 0)),
        ],
        out_specs=pl.BlockSpec((tm, D_MODEL), lambda i, f, be, nu: (i, 0)),
        scratch_shapes=[pltpu.VMEM((2, tm, D_MODEL), F32),
                        pltpu.VMEM((tm, D_MODEL), BF16),
                        pltpu.SemaphoreType.DMA((2,))],
    )
    return pl.pallas_call(
        _moe_kernel,
        name="moe_ffn",
        grid_spec=grid_spec,
        out_shape=jax.ShapeDtypeStruct((n_rows, D_MODEL), F32),
        compiler_params=_cparams(("arbitrary", "arbitrary")),
    )(block_e, n_used, tok3, tok3, x, wg, wu, wd)


COMB_TT = 256


def _combine_kernel(pos0_ref, pos1_ref, y_hbm, wt_ref, x_ref, lg_ref, lb_ref, xo_ref, xbo_ref,
                    g0_ref, g1_ref, sem):
    tt = COMB_TT
    _gather_rows(y_hbm, pos0_ref, g0_ref, sem.at[0], tt)
    _gather_rows(y_hbm, pos1_ref, g1_ref, sem.at[1], tt)
    _wait_rows(y_hbm, g0_ref, sem.at[0], tt)
    _wait_rows(y_hbm, g1_ref, sem.at[1], tt)
    wt = wt_ref[...]
    f = wt[:, 0:1] * g0_ref[...] + wt[:, 1:2] * g1_ref[...]
    y = _layer_norm(DN_ALPHA * x_ref[...] + f, lg_ref[...], lb_ref[...])
    xo_ref[...] = y
    xbo_ref[...] = y.astype(BF16)


def _combine(pos, yrows, wt, x, ln_g, ln_b, layer):
    n = x.shape[0]
    tt = COMB_TT
    pos0 = pos[0::TOP_K].reshape(n // tt, 1, tt)
    pos1 = pos[1::TOP_K].reshape(n // tt, 1, tt)
    row = pl.BlockSpec((tt, D_MODEL), lambda i: (i, 0))
    vec = pl.BlockSpec((None, 1, D_MODEL), lambda i: (2 * layer + 1, 0, 0))
    idx = pl.BlockSpec((None, 1, tt), lambda i: (i, 0, 0), memory_space=pltpu.SMEM)
    return pl.pallas_call(
        _combine_kernel,
        name="combine",
        grid=(n // tt,),
        in_specs=[idx, idx,
                  pl.BlockSpec(memory_space=pl.ANY),
                  pl.BlockSpec((tt, LANES), lambda i: (i, 0)),
                  row, vec, vec],
        out_specs=[row, row],
        out_shape=[jax.ShapeDtypeStruct((n, D_MODEL), F32),
                   jax.ShapeDtypeStruct((n, D_MODEL), BF16)],
        scratch_shapes=[pltpu.VMEM((tt, D_MODEL), F32),
                        pltpu.VMEM((tt, D_MODEL), F32),
                        pltpu.SemaphoreType.DMA((2,))],
        compiler_params=_cparams(("arbitrary",)),
    )(pos0, pos1, yrows, wt, x, ln_g, ln_b)


def _routing_plan(idx, n):
    n_assign = n * TOP_K
    e_flat = idx[:, :TOP_K].reshape(-1)
    onehot = (e_flat[:, None] == jnp.arange(N_EXPERTS, dtype=jnp.int32)[None, :]).astype(jnp.int32)
    csum = jnp.cumsum(onehot, axis=0)
    rank = jnp.sum(csum * onehot, axis=1) - 1
    counts = csum[-1]
    padded = (counts + MOE_TM - 1) // MOE_TM * MOE_TM
    pend = jnp.cumsum(padded)
    pstart = pend - padded
    pos = (jnp.sum(pstart[None, :] * onehot, axis=1) + rank).astype(jnp.int32)
    n_rows = (-(-n_assign // MOE_TM) + N_EXPERTS) * MOE_TM
    nb = n_rows // MOE_TM
    row_tok = jnp.zeros((n_rows,), jnp.int32).at[pos].set(
        jnp.arange(n_assign, dtype=jnp.int32) // TOP_K)
    block_start = jnp.arange(nb, dtype=jnp.int32) * MOE_TM
    block_e = jnp.minimum(jnp.sum((pend[None, :] <= block_start[:, None]).astype(jnp.int32), axis=1),
                          N_EXPERTS - 1).astype(jnp.int32)
    n_used = (pend[-1:] // MOE_TM).astype(jnp.int32)
    return pos, row_tok, block_e, n_used


def kernel(x, w_in, gla_a_up, gla_a_bias, gla_norm_g, gla_w_o, conv_w_dw, conv_b_dw, conv_ln_g, conv_ln_b, conv_w_pw, w_out, ln_g, ln_b, ffn_w_gate, ffn_w_up, ffn_w_down, moe_w_router, moe_w_gate, moe_w_up, moe_w_down):
    batch, seq, d = x.shape
    n = batch * seq
    xf = x.reshape(n, d)
    xb = xf.astype(BF16)
    a_lo, a_hi = QKVR, QKVR + GLA_RANK
    w_in_b = w_in.astype(BF16)
    w_rest = w_in_b[:, :, a_hi:]
    w_alow = jnp.pad(w_in_b[:, :, a_lo:a_hi], ((0, 0), (0, 0), (0, LANES - GLA_RANK)))
    wup = jnp.pad(gla_a_up, ((0, 0), (0, LANES - GLA_RANK), (0, 0)))
    w_o, w_pw, w_mix = gla_w_o.astype(BF16), conv_w_pw.astype(BF16), w_out.astype(BF16)
    ffn_g, ffn_u, ffn_d = ffn_w_gate.astype(BF16), ffn_w_up.astype(BF16), ffn_w_down.astype(BF16)
    moe_side = {(0, "mm"): moe_w_gate.reshape(-1, FFN_DIM), (1, "mm"): moe_w_up.reshape(-1, FFN_DIM),
                (0, "ffn"): moe_w_down.reshape(-1, D_MODEL)}
    moe_b = {}
    w_r = jnp.pad(moe_w_router, ((0, 0), (0, 0), (0, LANES - N_EXPERTS)))
    ba, gn = gla_a_bias[:, None, :], gla_norm_g[:, None, :]
    cb, cg, cbeta = conv_b_dw[:, None, :], conv_ln_g[:, None, :], conv_ln_b[:, None, :]
    lng = ln_g.reshape(2 * DEPTH, 1, d)
    lnb = ln_b.reshape(2 * DEPTH, 1, d)
    for layer in range(DEPTH):
        u1 = _matmul(xb, w_in_b, layer, BF16, 1024, 1024, n=QKVR)
        if (layer, "mm") in moe_side:
            u2, moe_b[layer, "mm"] = _matmul(xb, w_rest, layer, BF16, 1024, 1024,
                                             side=moe_side[layer, "mm"])
        else:
            u2 = _matmul(xb, w_rest, layer, BF16, 1024, 1024)
        a_low = _matmul(xb, w_alow, layer, F32, 1024, LANES)
        cum = _decay(a_low, wup, ba, layer)
        a1 = _gla(u1, cum, gn, layer, batch, seq)
        a2 = _conv(u2, conv_w_dw, cb, cg, cbeta, layer, batch, seq)
        xf = _merge(a1, a2, u2, xf, w_o, w_pw, w_mix, lng, lnb, layer)
        if layer % 2 == 0:
            if (layer, "ffn") in moe_side:
                xf, xb, moe_b[layer, "ffn"] = _ffn(xf, ffn_g, ffn_u, ffn_d, lng, lnb, layer,
                                                   side=moe_side[layer, "ffn"])
            else:
                xf, xb = _ffn(xf, ffn_g, ffn_u, ffn_d, lng, lnb, layer)
        else:
            moe_g = moe_b[0, "mm"].reshape(moe_w_gate.shape)
            moe_u = moe_b[1, "mm"].reshape(moe_w_up.shape)
            moe_d = moe_b[0, "ffn"].reshape(moe_w_down.shape)
            idx, wt = _router(xf, w_r, layer)
            pos, row_tok, block_e, n_used = _routing_plan(idx, n)
            yrows = _moe_ffn(block_e, n_used, row_tok, xf, moe_g, moe_u, moe_d, layer)
            xf, xb = _combine(pos, yrows, wt, xf, lng, lnb, layer)
    return xf.reshape(batch, seq, d)
```

```python
import jax
import jax.numpy as jnp
from jax import lax
from jax.experimental import pallas as pl
from jax.experimental.pallas import tpu as pltpu

D_MODEL = 2048
DEPTH = 4
GLA_HEADS = 4
GLA_DK = 256
GLA_DV = 512
GLA_RANK = 16
GLA_TAU = 16.0
GLA_CHUNK = 64
CONV_WIDTH = 31
FFN_DIM = 5632
N_EXPERTS = 8
TOP_K = 2
DN_ALPHA = (2 * DEPTH) ** 0.25
LN_EPS = 1e-5

LANES = 128
SUBLANES = 8
QKVR = 2 * GLA_HEADS * GLA_DK + 2 * GLA_HEADS * GLA_DV
VMEM_LIMIT = 56 * 1024 * 1024

F32 = jnp.float32
BF16 = jnp.bfloat16
NT = (((1,), (1,)), ((), ()))
TN = (((0,), (0,)), ((), ()))


def _cparams(sem):
    return pltpu.CompilerParams(dimension_semantics=sem, vmem_limit_bytes=VMEM_LIMIT)


def _dot(a, b):
    return jnp.dot(a, b, preferred_element_type=F32)


def _split_bf16(a):
    hi = a.astype(BF16)
    lo = (a - hi.astype(F32)).astype(BF16)
    return hi, lo


def _dot_3pass(a, b):
    a_hi, a_lo = _split_bf16(a)
    b_hi, b_lo = _split_bf16(b)
    return _dot(a_hi, b_hi) + (_dot(a_hi, b_lo) + _dot(a_lo, b_hi))


def _dot_exact_lhs(a_exact, b):
    b_hi, b_lo = _split_bf16(b)
    return _dot(a_exact, b_hi) + _dot(a_exact, b_lo)


def _sigmoid(x):
    return 1.0 / (1.0 + jnp.exp(-x))


def _layer_norm(y, g, b):
    mu = jnp.mean(y, axis=-1, keepdims=True)
    d = y - mu
    var = jnp.mean(d * d, axis=-1, keepdims=True)
    return d * lax.rsqrt(var + LN_EPS) * g + b


def _mm_kernel(a_ref, w_ref, o_ref):
    o_ref[...] = _dot(a_ref[...], w_ref[...]).astype(o_ref.dtype)


def _mm_cast_kernel(a_ref, w_ref, side_ref, o_ref, side_out_ref):
    o_ref[...] = _dot(a_ref[...], w_ref[...]).astype(o_ref.dtype)
    side_out_ref[...] = side_ref[...].astype(side_out_ref.dtype)


def _side_spec(side, n_steps, step_of):
    rows, cols = side.shape
    assert rows % n_steps == 0 and (rows // n_steps) % (2 * SUBLANES) == 0
    return pl.BlockSpec((rows // n_steps, cols), lambda *g: (step_of(*g), 0))


def _matmul(a, w_stack, layer, out_dtype, tm, tn, n=None, side=None):
    m, k = a.shape
    n = w_stack.shape[2] if n is None else n
    nj = n // tn
    in_specs = [pl.BlockSpec((tm, k), lambda i, j: (i, 0)),
                pl.BlockSpec((None, k, tn), lambda i, j: (layer, 0, j))]
    out_spec = pl.BlockSpec((tm, tn), lambda i, j: (i, j))
    out_shape = jax.ShapeDtypeStruct((m, n), out_dtype)
    if side is None:
        return pl.pallas_call(
            _mm_kernel, name="mm", grid=(m // tm, nj), in_specs=in_specs, out_specs=out_spec,
            out_shape=out_shape, compiler_params=_cparams(("parallel", "arbitrary")),
        )(a, w_stack)
    sspec = _side_spec(side, (m // tm) * nj, lambda i, j: i * nj + j)
    return pl.pallas_call(
        _mm_cast_kernel, name="mm_cast", grid=(m // tm, nj),
        in_specs=in_specs + [sspec], out_specs=[out_spec, sspec],
        out_shape=[out_shape, jax.ShapeDtypeStruct(side.shape, BF16)],
        compiler_params=_cparams(("arbitrary", "arbitrary")),
    )(a, w_stack, side)


GLA_TB = 256


def _decay_kernel(al_ref, wup_ref, ba_ref, cum_ref):
    c, tb = GLA_CHUNK, GLA_TB
    row = lax.broadcasted_iota(jnp.int32, (tb, tb), 0)
    col = lax.broadcasted_iota(jnp.int32, (tb, tb), 1)
    shift = c.bit_length() - 1
    same_chunk = lax.shift_right_logical(row, shift) == lax.shift_right_logical(col, shift)
    tril_blocks = (same_chunk & (row >= col)).astype(BF16)
    x = _dot_3pass(al_ref[...], wup_ref[...]) + ba_ref[...]
    log_a = (jnp.minimum(x, 0.0) - jnp.log1p(jnp.exp(-jnp.abs(x)))) / GLA_TAU
    cum_ref[...] = _dot_exact_lhs(tril_blocks, log_a)


def _decay(a_low, wup, ba, layer):
    n = a_low.shape[0]
    tb = GLA_TB
    width = GLA_HEADS * GLA_DK
    return pl.pallas_call(
        _decay_kernel,
        name="decay",
        grid=(n // tb,),
        in_specs=[pl.BlockSpec((tb, LANES), lambda i: (i, 0)),
                  pl.BlockSpec((None, LANES, width), lambda i: (layer, 0, 0)),
                  pl.BlockSpec((None, 1, width), lambda i: (layer, 0, 0))],
        out_specs=pl.BlockSpec((tb, width), lambda i: (i, 0)),
        out_shape=jax.ShapeDtypeStruct((n, width), F32),
        compiler_params=_cparams(("parallel",)),
    )(a_low, wup, ba)


def _rows_of(vectors, c):
    return jnp.concatenate([jnp.broadcast_to(v, (c, v.shape[-1])) for v in vectors], axis=0)


def _gla_kernel(q_ref, k_ref, v_ref, r_ref, cum_ref, g_ref, o_ref, st_ref):
    c, tb = GLA_CHUNK, GLA_TB
    nc = tb // c

    @pl.when(pl.program_id(2) == 0)
    def _():
        st_ref[...] = jnp.zeros_like(st_ref)

    cum = cum_ref[...]
    last = [cum[(ci + 1) * c - 1:(ci + 1) * c, :] for ci in range(nc)]
    before = [jnp.zeros_like(last[0])]
    for ci in range(nc):
        before.append(before[ci] + last[ci])
    total = before[nc]

    q = q_ref[...].astype(F32) * (GLA_DK ** -0.5)
    k = k_ref[...].astype(F32)
    q_in = q * jnp.exp(cum)
    k_in = (k * jnp.exp(-cum)).astype(BF16)
    k_st = k * jnp.exp(_rows_of(last, c) - cum)
    q_blk = (q_in * _rows_of([jnp.exp(b) for b in before[:nc]], c)).astype(BF16)
    k_blk = (k_st * _rows_of([jnp.exp(total - before[ci + 1]) for ci in range(nc)], c)
             ).astype(BF16)
    q_in = q_in.astype(BF16)
    k_st_b = k_st.astype(BF16)

    scores = []
    for ci in range(nc):
        keys = []
        for cj in range(nc):
            sj = slice(cj * c, (cj + 1) * c)
            if cj == ci:
                keys.append(k_in[sj])
            elif cj < ci - 1:
                keys.append((k_st[sj] * jnp.exp(before[ci] - before[cj + 1])).astype(BF16))
            else:
                keys.append(k_st_b[sj])
        scores.append(lax.dot_general(q_in[ci * c:(ci + 1) * c], jnp.concatenate(keys, axis=0),
                                      NT, preferred_element_type=F32))
    row = lax.broadcasted_iota(jnp.int32, (tb, tb), 0)
    col = lax.broadcasted_iota(jnp.int32, (tb, tb), 1)
    att = jnp.where(row >= col, jnp.concatenate(scores, axis=0), 0.0).astype(BF16)

    v = v_ref[...]
    st = st_ref[...]
    o = _dot(att, v) + lax.dot_general(q_blk, st.astype(BF16), NT, preferred_element_type=F32)
    st_ref[...] = jnp.exp(total) * st + lax.dot_general(v, k_blk, TN, preferred_element_type=F32)
    o = o * lax.rsqrt(jnp.mean(o * o, axis=-1, keepdims=True) + LN_EPS) * g_ref[...]
    r = r_ref[...].astype(F32)
    o_ref[...] = (o * (r * _sigmoid(r))).astype(o_ref.dtype)


def _gla(u, cum, g_norm, layer, batch, seq):
    n = u.shape[0]
    tb = GLA_TB
    nt = seq // tb
    rows = lambda b, h, t: b * nt + t
    kq = GLA_HEADS
    kv = 2 * GLA_HEADS * GLA_DK // GLA_DV
    kr = kv + GLA_HEADS
    return pl.pallas_call(
        _gla_kernel,
        name="gla",
        grid=(batch, GLA_HEADS, nt),
        in_specs=[
            pl.BlockSpec((tb, GLA_DK), lambda b, h, t: (rows(b, h, t), h)),
            pl.BlockSpec((tb, GLA_DK), lambda b, h, t: (rows(b, h, t), kq + h)),
            pl.BlockSpec((tb, GLA_DV), lambda b, h, t: (rows(b, h, t), kv + h)),
            pl.BlockSpec((tb, GLA_DV), lambda b, h, t: (rows(b, h, t), kr + h)),
            pl.BlockSpec((tb, GLA_DK), lambda b, h, t: (rows(b, h, t), h)),
            pl.BlockSpec((None, 1, GLA_DV), lambda b, h, t: (layer, 0, 0)),
        ],
        out_specs=pl.BlockSpec((tb, GLA_DV), lambda b, h, t: (rows(b, h, t), h)),
        out_shape=jax.ShapeDtypeStruct((n, GLA_HEADS * GLA_DV), BF16),
        scratch_shapes=[pltpu.VMEM((GLA_DV, GLA_DK), F32)],
        compiler_params=_cparams(("parallel", "parallel", "arbitrary")),
    )(u, u, u, u, cum, g_norm)


CONV_TT = 256
CONV_HALO = 32
CONV_RB = 128


def _conv_kernel(val_ref, gate_ref, w_ref, b_ref, lg_ref, lb_ref, o_ref, h_ref, c_ref):
    tt, halo, rb = CONV_TT, CONV_HALO, CONV_RB

    @pl.when(pl.program_id(1) == 0)
    def _():
        h_ref[pl.ds(0, halo), :] = jnp.zeros((halo, D_MODEL), F32)

    h_ref[pl.ds(halo, tt), :] = val_ref[...].astype(F32) * _sigmoid(gate_ref[...].astype(F32))

    def lane_block(li, carry):
        ls = pl.ds(pl.multiple_of(li * LANES, LANES), LANES)
        for r0 in range(0, tt, rb):
            win = h_ref[pl.ds(r0, rb + halo), ls]
            acc = jnp.broadcast_to(b_ref[:, ls], (rb, LANES))
            for b in range(SUBLANES):
                shifted = win if b == 0 else pltpu.roll(win, b, axis=0)
                for a in range(halo // SUBLANES):
                    d = SUBLANES * a + b
                    if d > CONV_WIDTH - 1:
                        continue
                    j = CONV_WIDTH - 1 - d
                    lo = halo - SUBLANES * a
                    acc = acc + w_ref[j:j + 1, ls] * shifted[lo:lo + rb, :]
            c_ref[pl.ds(r0, rb), ls] = acc
        return carry

    lax.fori_loop(0, D_MODEL // LANES, lane_block, 0)
    h_ref[pl.ds(0, halo), :] = h_ref[pl.ds(tt, halo), :]
    y = _layer_norm(c_ref[...], lg_ref[...], lb_ref[...])
    o_ref[...] = (y * _sigmoid(y)).astype(o_ref.dtype)


def _conv(u2, w_dw, b_dw, ln_g, ln_b, layer, batch, seq):
    n = u2.shape[0]
    tt = CONV_TT
    nt = seq // tt
    vec = pl.BlockSpec((None, 1, D_MODEL), lambda b, t: (layer, 0, 0))
    return pl.pallas_call(
        _conv_kernel,
        name="conv",
        grid=(batch, nt),
        in_specs=[
            pl.BlockSpec((tt, D_MODEL), lambda b, t: (b * nt + t, 0)),
            pl.BlockSpec((tt, D_MODEL), lambda b, t: (b * nt + t, 1)),
            pl.BlockSpec((None, CONV_WIDTH, D_MODEL), lambda b, t: (layer, 0, 0)),
            vec, vec, vec,
        ],
        out_specs=pl.BlockSpec((tt, D_MODEL), lambda b, t: (b * nt + t, 0)),
        out_shape=jax.ShapeDtypeStruct((n, D_MODEL), BF16),
        scratch_shapes=[pltpu.VMEM((tt + CONV_HALO, D_MODEL), F32),
                        pltpu.VMEM((tt, D_MODEL), F32)],
        compiler_params=_cparams(("parallel", "arbitrary")),
    )(u2, u2, w_dw, b_dw, ln_g, ln_b)


MERGE_TM = 256


def _merge_kernel(a1_ref, a2_ref, ga_ref, gb_ref, x_ref, wo_ref, wpw_ref, wout_ref,
                  lg_ref, lb_ref, xo_ref):
    ya = _dot(a1_ref[...], wo_ref[...])
    yb = _dot(a2_ref[...], wpw_ref[...])
    z = _sigmoid(ga_ref[...].astype(F32)) * ya + _sigmoid(gb_ref[...].astype(F32)) * yb
    mix = _dot(z.astype(BF16), wout_ref[...])
    xo_ref[...] = _layer_norm(DN_ALPHA * x_ref[...] + mix, lg_ref[...], lb_ref[...])


def _merge(a1, a2, u2, x, w_o, w_pw, w_out, ln_g, ln_b, layer):
    n = x.shape[0]
    tm = MERGE_TM
    row = lambda c: pl.BlockSpec((tm, D_MODEL), lambda i: (i, c))
    wfull = pl.BlockSpec((None, D_MODEL, D_MODEL), lambda i: (layer, 0, 0),
                         pipeline_mode=pl.Buffered(1))
    vec = pl.BlockSpec((None, 1, D_MODEL), lambda i: (2 * layer, 0, 0))
    return pl.pallas_call(
        _merge_kernel,
        name="merge",
        grid=(n // tm,),
        in_specs=[row(0), row(0), row(2), row(3), row(0), wfull, wfull, wfull, vec, vec],
        out_specs=row(0),
        out_shape=jax.ShapeDtypeStruct((n, D_MODEL), F32),
        compiler_params=_cparams(("parallel",)),
    )(a1, a2, u2, u2, x, w_o, w_pw, w_out, ln_g, ln_b)


FFN_TM = 512
FFN_TF = 512
FFN_SPLIT = 2


def _swiglu_accumulate(xb_ref, wg_ref, wu_ref, wd_ref, acc_ref, tm):
    rows = tm // FFN_SPLIT
    wg, wu, wd = wg_ref[...], wu_ref[...], wd_ref[...]
    for s in range(FFN_SPLIT):
        sl = pl.ds(s * rows, rows)
        xb = xb_ref[sl, :]
        g = _dot(xb, wg)
        u = _dot(xb, wu)
        h = (g * _sigmoid(g) * u).astype(BF16)
        acc_ref[sl, :] += _dot(h, wd)


def _ffn_cast_kernel(wg_ref, wu_ref, wd_ref, x_ref, lg_ref, lb_ref, side_ref,
                     xo_ref, xbo_ref, side_out_ref, xb_ref, acc_ref):
    side_out_ref[...] = side_ref[...].astype(side_out_ref.dtype)
    _ffn_kernel(wg_ref, wu_ref, wd_ref, x_ref, lg_ref, lb_ref, xo_ref, xbo_ref, xb_ref, acc_ref)


def _ffn_kernel(wg_ref, wu_ref, wd_ref, x_ref, lg_ref, lb_ref, xo_ref, xbo_ref, xb_ref, acc_ref):
    f = pl.program_id(1)

    @pl.when(f == 0)
    def _():
        xb_ref[...] = x_ref[...].astype(BF16)
        acc_ref[...] = jnp.zeros_like(acc_ref)

    _swiglu_accumulate(xb_ref, wg_ref, wu_ref, wd_ref, acc_ref, FFN_TM)

    @pl.when(f == pl.num_programs(1) - 1)
    def _():
        y = _layer_norm(DN_ALPHA * x_ref[...] + acc_ref[...], lg_ref[...], lb_ref[...])
        xo_ref[...] = y
        xbo_ref[...] = y.astype(BF16)


def _ffn(x, wg, wu, wd, ln_g, ln_b, layer, side=None):
    n = x.shape[0]
    tm, tf = FFN_TM, FFN_TF
    il = layer // 2
    nf = FFN_DIM // tf
    row = pl.BlockSpec((tm, D_MODEL), lambda i, f: (i, 0))
    vec = pl.BlockSpec((None, 1, D_MODEL), lambda i, f: (2 * layer + 1, 0, 0))
    in_specs = [pl.BlockSpec((None, D_MODEL, tf), lambda i, f: (il, 0, f)),
                pl.BlockSpec((None, D_MODEL, tf), lambda i, f: (il, 0, f)),
                pl.BlockSpec((None, tf, D_MODEL), lambda i, f: (il, f, 0)),
                row, vec, vec]
    out_specs = [row, row]
    out_shape = [jax.ShapeDtypeStruct((n, D_MODEL), F32), jax.ShapeDtypeStruct((n, D_MODEL), BF16)]
    scratch = [pltpu.VMEM((tm, D_MODEL), BF16), pltpu.VMEM((tm, D_MODEL), F32)]
    if side is None:
        return pl.pallas_call(
            _ffn_kernel, name="ffn", grid=(n // tm, nf), in_specs=in_specs, out_specs=out_specs,
            out_shape=out_shape, scratch_shapes=scratch,
            compiler_params=_cparams(("parallel", "arbitrary")),
        )(wg, wu, wd, x, ln_g, ln_b)
    sspec = _side_spec(side, (n // tm) * nf, lambda i, f: i * nf + f)
    return pl.pallas_call(
        _ffn_cast_kernel, name="ffn_cast", grid=(n // tm, nf),
        in_specs=in_specs + [sspec], out_specs=out_specs + [sspec],
        out_shape=out_shape + [jax.ShapeDtypeStruct(side.shape, BF16)], scratch_shapes=scratch,
        compiler_params=_cparams(("arbitrary", "arbitrary")),
    )(wg, wu, wd, x, ln_g, ln_b, side)


ROUTER_TM = 1024


def _router_kernel(x_ref, w_ref, idx_ref, wt_ref):
    logits = _dot_3pass(x_ref[...], w_ref[...])
    lane = lax.broadcasted_iota(jnp.int32, logits.shape, 1)
    neg = jnp.float32(-jnp.inf)
    lg = jnp.where(lane < N_EXPERTS, logits, neg)
    m1 = jnp.max(lg, axis=-1, keepdims=True)
    i1 = jnp.min(jnp.where(lg == m1, lane, LANES), axis=-1, keepdims=True)
    lg2 = jnp.where(lane == i1, neg, lg)
    m2 = jnp.max(lg2, axis=-1, keepdims=True)
    i2 = jnp.min(jnp.where(lg2 == m2, lane, LANES), axis=-1, keepdims=True)
    e2 = jnp.exp(m2 - m1)
    w1 = 1.0 / (1.0 + e2)
    w2 = e2 / (1.0 + e2)
    idx_ref[...] = jnp.where(lane == 0, i1, jnp.where(lane == 1, i2, 0))
    wt_ref[...] = jnp.where(lane == 0, w1, jnp.where(lane == 1, w2, 0.0))


def _router(x, w_router_padded, layer):
    n = x.shape[0]
    tm = ROUTER_TM
    out = pl.BlockSpec((tm, LANES), lambda i: (i, 0))
    return pl.pallas_call(
        _router_kernel,
        name="router",
        grid=(n // tm,),
        in_specs=[pl.BlockSpec((tm, D_MODEL), lambda i: (i, 0)),
                  pl.BlockSpec((None, D_MODEL, LANES), lambda i: (layer // 2, 0, 0))],
        out_specs=[out, out],
        out_shape=[jax.ShapeDtypeStruct((n, LANES), jnp.int32),
                   jax.ShapeDtypeStruct((n, LANES), F32)],
        compiler_params=_cparams(("parallel",)),
    )(x, w_router_padded)


MOE_TM = 512
MOE_TF = 512
GATHER_UNROLL = 8


def _row_copy(src_hbm, dst_vmem, sem, src_row, dst_row):
    return pltpu.make_async_copy(src_hbm.at[pl.ds(src_row, 1), :],
                                 dst_vmem.at[pl.ds(dst_row, 1), :], sem)


def _gather_rows(src_hbm, idx_ref, dst_vmem, sem, n_rows):
    def issue(r, carry):
        _row_copy(src_hbm, dst_vmem, sem, idx_ref[0, r], r).start()
        return carry
    lax.fori_loop(0, n_rows, issue, 0, unroll=GATHER_UNROLL)


def _wait_rows(src_hbm, dst_vmem, sem, n_rows):
    def drain(r, carry):
        _row_copy(src_hbm, dst_vmem, sem, 0, r).wait()
        return carry
    lax.fori_loop(0, n_rows, drain, 0, unroll=GATHER_UNROLL)


def _moe_kernel(be_ref, nu_ref, tok_ref, tokn_ref, x_hbm, wg_ref, wu_ref, wd_ref, y_ref,
                xf_ref, xb_ref, sem):
    i = pl.program_id(0)
    f = pl.program_id(1)
    last = pl.num_programs(1) - 1
    n_used = nu_ref[0]
    used = i < n_used
    slot = lax.rem(i, 2)

    @pl.when(used & (f == 0))
    def _():
        @pl.when(i == 0)
        def _():
            _gather_rows(x_hbm, tok_ref, xf_ref.at[0], sem.at[0], MOE_TM)

        _wait_rows(x_hbm, xf_ref.at[slot], sem.at[slot], MOE_TM)
        xb_ref[...] = xf_ref[slot].astype(BF16)
        y_ref[...] = jnp.zeros_like(y_ref)

        @pl.when(i + 1 < n_used)
        def _():
            _gather_rows(x_hbm, tokn_ref, xf_ref.at[1 - slot], sem.at[1 - slot], MOE_TM)

    @pl.when(used)
    def _():
        _swiglu_accumulate(xb_ref, wg_ref, wu_ref, wd_ref, y_ref, MOE_TM)

    @pl.when(jnp.logical_not(used) & (f == last))
    def _():
        y_ref[...] = jnp.zeros_like(y_ref)


def _moe_ffn(block_e, n_used, row_tok, x, wg, wu, wd, layer):
    n_rows = row_tok.shape[0]
    tm, tf = MOE_TM, MOE_TF
    nb = n_rows // tm
    nf = FFN_DIM // tf
    il = layer // 2
    tok3 = row_tok.reshape(nb, 1, tm)

    def fsel(i, f, nu):
        return jnp.where(i < nu[0], f, nf - 1)

    grid_spec = pltpu.PrefetchScalarGridSpec(
        num_scalar_prefetch=2,
        grid=(nb, nf),
        in_specs=[
            pl.BlockSpec((None, 1, tm), lambda i, f, be, nu: (i, 0, 0), memory_space=pltpu.SMEM),
            pl.BlockSpec((None, 1, tm), lambda i, f, be, nu: (jnp.minimum(i + 1, nb - 1), 0, 0),
                         memory_space=pltpu.SMEM),
            pl.BlockSpec(memory_space=pl.ANY),
            pl.BlockSpec((None, None, D_MODEL, tf),
                         lambda i, f, be, nu: (il, be[i], 0, fsel(i, f, nu))),
            pl.BlockSpec((None, None, D_MODEL, tf),
                         lambda i, f, be, nu: (il, be[i], 0, fsel(i, f, nu))),
            pl.BlockSpec((None, None, tf, D_MODEL),
                         lambda i, f, be, nu: (il, be[i], fsel(i, f, nu), 0)),
        ],
        out_specs=pl.BlockSpec((tm, D_MODEL), lambda i, f, be, nu: (i, 0)),
        scratch_shapes=[pltpu.VMEM((2, tm, D_MODEL), F32),
                        pltpu.VMEM((tm, D_MODEL), BF16),
                        pltpu.SemaphoreType.DMA((2,))],
    )
    return pl.pallas_call(
        _moe_kernel,
        name="moe_ffn",
        grid_spec=grid_spec,
        out_shape=jax.ShapeDtypeStruct((n_rows, D_MODEL), F32),
        compiler_params=_cparams(("arbitrary", "arbitrary")),
    )(block_e, n_used, tok3, tok3, x, wg, wu, wd)


COMB_TT = 256


def _combine_kernel(pos0_ref, pos1_ref, pos0n_ref, pos1n_ref, y_hbm, wt_ref, x_ref, lg_ref, lb_ref,
                    xo_ref, xbo_ref, g_ref, sem):
    tt = COMB_TT
    i = pl.program_id(0)
    slot = lax.rem(i, 2)

    @pl.when(i == 0)
    def _():
        _gather_rows(y_hbm, pos0_ref, g_ref.at[0, 0], sem.at[0, 0], tt)
        _gather_rows(y_hbm, pos1_ref, g_ref.at[0, 1], sem.at[0, 1], tt)

    @pl.when(i + 1 < pl.num_programs(0))
    def _():
        _gather_rows(y_hbm, pos0n_ref, g_ref.at[1 - slot, 0], sem.at[1 - slot, 0], tt)
        _gather_rows(y_hbm, pos1n_ref, g_ref.at[1 - slot, 1], sem.at[1 - slot, 1], tt)

    _wait_rows(y_hbm, g_ref.at[slot, 0], sem.at[slot, 0], tt)
    _wait_rows(y_hbm, g_ref.at[slot, 1], sem.at[slot, 1], tt)
    wt = wt_ref[...]
    f = wt[:, 0:1] * g_ref[slot, 0] + wt[:, 1:2] * g_ref[slot, 1]
    y = _layer_norm(DN_ALPHA * x_ref[...] + f, lg_ref[...], lb_ref[...])
    xo_ref[...] = y
    xbo_ref[...] = y.astype(BF16)


def _combine(pos, yrows, wt, x, ln_g, ln_b, layer):
    n = x.shape[0]
    tt = COMB_TT
    pos0 = pos[0::TOP_K].reshape(n // tt, 1, tt)
    pos1 = pos[1::TOP_K].reshape(n // tt, 1, tt)
    row = pl.BlockSpec((tt, D_MODEL), lambda i: (i, 0))
    vec = pl.BlockSpec((None, 1, D_MODEL), lambda i: (2 * layer + 1, 0, 0))
    nt = n // tt
    idx = pl.BlockSpec((None, 1, tt), lambda i: (i, 0, 0), memory_space=pltpu.SMEM)
    idx_next = pl.BlockSpec((None, 1, tt), lambda i: (jnp.minimum(i + 1, nt - 1), 0, 0),
                            memory_space=pltpu.SMEM)
    return pl.pallas_call(
        _combine_kernel,
        name="combine",
        grid=(nt,),
        in_specs=[idx, idx, idx_next, idx_next,
                  pl.BlockSpec(memory_space=pl.ANY),
                  pl.BlockSpec((tt, LANES), lambda i: (i, 0)),
                  row, vec, vec],
        out_specs=[row, row],
        out_shape=[jax.ShapeDtypeStruct((n, D_MODEL), F32),
                   jax.ShapeDtypeStruct((n, D_MODEL), BF16)],
        scratch_shapes=[pltpu.VMEM((2, TOP_K, tt, D_MODEL), F32),
                        pltpu.SemaphoreType.DMA((2, TOP_K))],
        compiler_params=_cparams(("arbitrary",)),
    )(pos0, pos1, pos0, pos1, yrows, wt, x, ln_g, ln_b)


def _routing_plan(idx, n):
    n_assign = n * TOP_K
    e_flat = idx[:, :TOP_K].reshape(-1)
    onehot = (e_flat[:, None] == jnp.arange(N_EXPERTS, dtype=jnp.int32)[None, :]).astype(jnp.int32)
    csum = jnp.cumsum(onehot, axis=0)
    rank = jnp.sum(csum * onehot, axis=1) - 1
    counts = csum[-1]
    padded = (counts + MOE_TM - 1) // MOE_TM * MOE_TM
    pend = jnp.cumsum(padded)
    pstart = pend - padded
    pos = (jnp.sum(pstart[None, :] * onehot, axis=1) + rank).astype(jnp.int32)
    n_rows = (-(-n_assign // MOE_TM) + N_EXPERTS) * MOE_TM
    nb = n_rows // MOE_TM
    row_tok = jnp.zeros((n_rows,), jnp.int32).at[pos].set(
        jnp.arange(n_assign, dtype=jnp.int32) // TOP_K, unique_indices=True,
        mode="promise_in_bounds")
    block_start = jnp.arange(nb, dtype=jnp.int32) * MOE_TM
    block_e = jnp.minimum(jnp.sum((pend[None, :] <= block_start[:, None]).astype(jnp.int32), axis=1),
                          N_EXPERTS - 1).astype(jnp.int32)
    n_used = (pend[-1:] // MOE_TM).astype(jnp.int32)
    return pos, row_tok, block_e, n_used


def kernel(x, w_in, gla_a_up, gla_a_bias, gla_norm_g, gla_w_o, conv_w_dw, conv_b_dw, conv_ln_g, conv_ln_b, conv_w_pw, w_out, ln_g, ln_b, ffn_w_gate, ffn_w_up, ffn_w_down, moe_w_router, moe_w_gate, moe_w_up, moe_w_down):
    batch, seq, d = x.shape
    n = batch * seq
    xf = x.reshape(n, d)
    xb = xf.astype(BF16)
    a_lo, a_hi = QKVR, QKVR + GLA_RANK
    w_in_b = w_in.astype(BF16)
    w_rest = w_in_b[:, :, a_hi:]
    w_alow = jnp.pad(w_in_b[:, :, a_lo:a_hi], ((0, 0), (0, 0), (0, LANES - GLA_RANK)))
    wup = jnp.pad(gla_a_up, ((0, 0), (0, LANES - GLA_RANK), (0, 0)))
    w_o, w_pw, w_mix = gla_w_o.astype(BF16), conv_w_pw.astype(BF16), w_out.astype(BF16)
    ffn_g, ffn_u, ffn_d = ffn_w_gate.astype(BF16), ffn_w_up.astype(BF16), ffn_w_down.astype(BF16)
    moe_side = {(0, "mm"): moe_w_gate.reshape(-1, FFN_DIM), (1, "mm"): moe_w_up.reshape(-1, FFN_DIM),
                (0, "ffn"): moe_w_down.reshape(-1, D_MODEL)}
    moe_b = {}
    w_r = jnp.pad(moe_w_router, ((0, 0), (0, 0), (0, LANES - N_EXPERTS)))
    ba, gn = gla_a_bias[:, None, :], gla_norm_g[:, None, :]
    cb, cg, cbeta = conv_b_dw[:, None, :], conv_ln_g[:, None, :], conv_ln_b[:, None, :]
    lng = ln_g.reshape(2 * DEPTH, 1, d)
    lnb = ln_b.reshape(2 * DEPTH, 1, d)
    for layer in range(DEPTH):
        u1 = _matmul(xb, w_in_b, layer, BF16, 1024, 1024, n=QKVR)
        if (layer, "mm") in moe_side:
            u2, moe_b[layer, "mm"] = _matmul(xb, w_rest, layer, BF16, 1024, 1024,
                                             side=moe_side[layer, "mm"])
        else:
            u2 = _matmul(xb, w_rest, layer, BF16, 1024, 1024)
        a_low = _matmul(xb, w_alow, layer, F32, 1024, LANES)
        cum = _decay(a_low, wup, ba, layer)
        a1 = _gla(u1, cum, gn, layer, batch, seq)
        a2 = _conv(u2, conv_w_dw, cb, cg, cbeta, layer, batch, seq)
        xf = _merge(a1, a2, u2, xf, w_o, w_pw, w_mix, lng, lnb, layer)
        if layer % 2 == 0:
            if (layer, "ffn") in moe_side:
                xf, xb, moe_b[layer, "ffn"] = _ffn(xf, ffn_g, ffn_u, ffn_d, lng, lnb, layer,
                                                   side=moe_side[layer, "ffn"])
            else:
                xf, xb = _ffn(xf, ffn_g, ffn_u, ffn_d, lng, lnb, layer)
        else:
            moe_g = moe_b[0, "mm"].reshape(moe_w_gate.shape)
            moe_u = moe_b[1, "mm"].reshape(moe_w_up.shape)
            moe_d = moe_b[0, "ffn"].reshape(moe_w_down.shape)
            idx, wt = _router(xf, w_r, layer)
            pos, row_tok, block_e, n_used = _routing_plan(idx, n)
            yrows = _moe_ffn(block_e, n_used, row_tok, xf, moe_g, moe_u, moe_d, layer)
            xf, xb = _combine(pos, yrows, wt, xf, lng, lnb, layer)
    return xf.reshape(batch, seq, d)
```

```python
import jax
import jax.numpy as jnp
from jax import lax
from jax.experimental import pallas as pl
from jax.experimental.pallas import tpu as pltpu

D_MODEL = 2048
DEPTH = 4
GLA_HEADS = 4
GLA_DK = 256
GLA_DV = 512
GLA_RANK = 16
GLA_TAU = 16.0
GLA_CHUNK = 64
CONV_WIDTH = 31
FFN_DIM = 5632
N_EXPERTS = 8
TOP_K = 2
DN_ALPHA = (2 * DEPTH) ** 0.25
LN_EPS = 1e-5

LANES = 128
SUBLANES = 8
QKVR = 2 * GLA_HEADS * GLA_DK + 2 * GLA_HEADS * GLA_DV
VMEM_LIMIT = 56 * 1024 * 1024

F32 = jnp.float32
BF16 = jnp.bfloat16
NT = (((1,), (1,)), ((), ()))
TN = (((0,), (0,)), ((), ()))


def _cparams(sem):
    return pltpu.CompilerParams(dimension_semantics=sem, vmem_limit_bytes=VMEM_LIMIT)


def _dot(a, b):
    return jnp.dot(a, b, preferred_element_type=F32)


def _split_bf16(a):
    hi = a.astype(BF16)
    lo = (a - hi.astype(F32)).astype(BF16)
    return hi, lo


def _dot_3pass(a, b):
    a_hi, a_lo = _split_bf16(a)
    b_hi, b_lo = _split_bf16(b)
    return _dot(a_hi, b_hi) + (_dot(a_hi, b_lo) + _dot(a_lo, b_hi))


def _dot_exact_lhs(a_exact, b):
    b_hi, b_lo = _split_bf16(b)
    return _dot(a_exact, b_hi) + _dot(a_exact, b_lo)


def _sigmoid(x):
    return 1.0 / (1.0 + jnp.exp(-x))


def _layer_norm(y, g, b):
    mu = jnp.mean(y, axis=-1, keepdims=True)
    d = y - mu
    var = jnp.mean(d * d, axis=-1, keepdims=True)
    return d * lax.rsqrt(var + LN_EPS) * g + b


def _mm_kernel(a_ref, w_ref, o_ref):
    o_ref[...] = _dot(a_ref[...], w_ref[...]).astype(o_ref.dtype)


def _mm_cast_kernel(a_ref, w_ref, side_ref, o_ref, side_out_ref):
    o_ref[...] = _dot(a_ref[...], w_ref[...]).astype(o_ref.dtype)
    side_out_ref[...] = side_ref[...].astype(side_out_ref.dtype)


def _side_spec(side, n_steps, step_of):
    rows, cols = side.shape
    assert rows % n_steps == 0 and (rows // n_steps) % (2 * SUBLANES) == 0
    return pl.BlockSpec((rows // n_steps, cols), lambda *g: (step_of(*g), 0))


def _matmul(a, w_stack, layer, out_dtype, tm, tn, n=None, side=None):
    m, k = a.shape
    n = w_stack.shape[2] if n is None else n
    nj = n // tn
    in_specs = [pl.BlockSpec((tm, k), lambda i, j: (i, 0)),
                pl.BlockSpec((None, k, tn), lambda i, j: (layer, 0, j))]
    out_spec = pl.BlockSpec((tm, tn), lambda i, j: (i, j))
    out_shape = jax.ShapeDtypeStruct((m, n), out_dtype)
    if side is None:
        return pl.pallas_call(
            _mm_kernel, name="mm", grid=(m // tm, nj), in_specs=in_specs, out_specs=out_spec,
            out_shape=out_shape, compiler_params=_cparams(("parallel", "arbitrary")),
        )(a, w_stack)
    sspec = _side_spec(side, (m // tm) * nj, lambda i, j: i * nj + j)
    return pl.pallas_call(
        _mm_cast_kernel, name="mm_cast", grid=(m // tm, nj),
        in_specs=in_specs + [sspec], out_specs=[out_spec, sspec],
        out_shape=[out_shape, jax.ShapeDtypeStruct(side.shape, BF16)],
        compiler_params=_cparams(("arbitrary", "arbitrary")),
    )(a, w_stack, side)


GLA_TB = 256


def _decay_kernel(al_ref, wup_ref, ba_ref, cum_ref):
    c, tb = GLA_CHUNK, GLA_TB
    row = lax.broadcasted_iota(jnp.int32, (tb, tb), 0)
    col = lax.broadcasted_iota(jnp.int32, (tb, tb), 1)
    shift = c.bit_length() - 1
    same_chunk = lax.shift_right_logical(row, shift) == lax.shift_right_logical(col, shift)
    tril_blocks = (same_chunk & (row >= col)).astype(BF16)
    x = _dot_3pass(al_ref[...], wup_ref[...]) + ba_ref[...]
    log_a = (jnp.minimum(x, 0.0) - jnp.log1p(jnp.exp(-jnp.abs(x)))) / GLA_TAU
    cum_ref[...] = _dot_exact_lhs(tril_blocks, log_a)


def _decay(a_low, wup, ba, layer):
    n = a_low.shape[0]
    tb = GLA_TB
    width = GLA_HEADS * GLA_DK
    return pl.pallas_call(
        _decay_kernel,
        name="decay",
        grid=(n // tb,),
        in_specs=[pl.BlockSpec((tb, LANES), lambda i: (i, 0)),
                  pl.BlockSpec((None, LANES, width), lambda i: (layer, 0, 0)),
                  pl.BlockSpec((None, 1, width), lambda i: (layer, 0, 0))],
        out_specs=pl.BlockSpec((tb, width), lambda i: (i, 0)),
        out_shape=jax.ShapeDtypeStruct((n, width), F32),
        compiler_params=_cparams(("parallel",)),
    )(a_low, wup, ba)


def _rows_of(vectors, c):
    return jnp.concatenate([jnp.broadcast_to(v, (c, v.shape[-1])) for v in vectors], axis=0)


def _gla_kernel(q_ref, k_ref, v_ref, r_ref, cum_ref, g_ref, o_ref, st_ref):
    c, tb = GLA_CHUNK, GLA_TB
    nc = tb // c

    @pl.when(pl.program_id(2) == 0)
    def _():
        st_ref[...] = jnp.zeros_like(st_ref)

    cum = cum_ref[...]
    last = [cum[(ci + 1) * c - 1:(ci + 1) * c, :] for ci in range(nc)]
    before = [jnp.zeros_like(last[0])]
    for ci in range(nc):
        before.append(before[ci] + last[ci])
    total = before[nc]

    q = q_ref[...].astype(F32) * (GLA_DK ** -0.5)
    k = k_ref[...].astype(F32)
    q_in = q * jnp.exp(cum)
    k_in = (k * jnp.exp(-cum)).astype(BF16)
    k_st = k * jnp.exp(_rows_of(last, c) - cum)
    q_blk = (q_in * _rows_of([jnp.exp(b) for b in before[:nc]], c)).astype(BF16)
    k_blk = (k_st * _rows_of([jnp.exp(total - before[ci + 1]) for ci in range(nc)], c)
             ).astype(BF16)
    q_in = q_in.astype(BF16)
    k_st_b = k_st.astype(BF16)

    scores = []
    for ci in range(nc):
        keys = []
        for cj in range(nc):
            sj = slice(cj * c, (cj + 1) * c)
            if cj == ci:
                keys.append(k_in[sj])
            elif cj < ci - 1:
                keys.append((k_st[sj] * jnp.exp(before[ci] - before[cj + 1])).astype(BF16))
            else:
                keys.append(k_st_b[sj])
        scores.append(lax.dot_general(q_in[ci * c:(ci + 1) * c], jnp.concatenate(keys, axis=0),
                                      NT, preferred_element_type=F32))
    row = lax.broadcasted_iota(jnp.int32, (tb, tb), 0)
    col = lax.broadcasted_iota(jnp.int32, (tb, tb), 1)
    att = jnp.where(row >= col, jnp.concatenate(scores, axis=0), 0.0).astype(BF16)

    v = v_ref[...]
    st = st_ref[...]
    o = _dot(att, v) + lax.dot_general(q_blk, st.astype(BF16), NT, preferred_element_type=F32)
    st_ref[...] = jnp.exp(total) * st + lax.dot_general(v, k_blk, TN, preferred_element_type=F32)
    o = o * lax.rsqrt(jnp.mean(o * o, axis=-1, keepdims=True) + LN_EPS) * g_ref[...]
    r = r_ref[...].astype(F32)
    o_ref[...] = (o * (r * _sigmoid(r))).astype(o_ref.dtype)


def _gla(u, cum, g_norm, layer, batch, seq):
    n = u.shape[0]
    tb = GLA_TB
    nt = seq // tb
    rows = lambda b, h, t: b * nt + t
    kq = GLA_HEADS
    kv = 2 * GLA_HEADS * GLA_DK // GLA_DV
    kr = kv + GLA_HEADS
    return pl.pallas_call(
        _gla_kernel,
        name="gla",
        grid=(batch, GLA_HEADS, nt),
        in_specs=[
            pl.BlockSpec((tb, GLA_DK), lambda b, h, t: (rows(b, h, t), h)),
            pl.BlockSpec((tb, GLA_DK), lambda b, h, t: (rows(b, h, t), kq + h)),
            pl.BlockSpec((tb, GLA_DV), lambda b, h, t: (rows(b, h, t), kv + h)),
            pl.BlockSpec((tb, GLA_DV), lambda b, h, t: (rows(b, h, t), kr + h)),
            pl.BlockSpec((tb, GLA_DK), lambda b, h, t: (rows(b, h, t), h)),
            pl.BlockSpec((None, 1, GLA_DV), lambda b, h, t: (layer, 0, 0)),
        ],
        out_specs=pl.BlockSpec((tb, GLA_DV), lambda b, h, t: (rows(b, h, t), h)),
        out_shape=jax.ShapeDtypeStruct((n, GLA_HEADS * GLA_DV), BF16),
        scratch_shapes=[pltpu.VMEM((GLA_DV, GLA_DK), F32)],
        compiler_params=_cparams(("parallel", "parallel", "arbitrary")),
    )(u, u, u, u, cum, g_norm)


CONV_TT = 256
CONV_HALO = 32
CONV_RB = 128


def _conv_kernel(val_ref, gate_ref, w_ref, b_ref, lg_ref, lb_ref, o_ref, h_ref, c_ref):
    tt, halo, rb = CONV_TT, CONV_HALO, CONV_RB

    @pl.when(pl.program_id(1) == 0)
    def _():
        h_ref[pl.ds(0, halo), :] = jnp.zeros((halo, D_MODEL), F32)

    h_ref[pl.ds(halo, tt), :] = val_ref[...].astype(F32) * _sigmoid(gate_ref[...].astype(F32))

    def lane_block(li, carry):
        ls = pl.ds(pl.multiple_of(li * LANES, LANES), LANES)
        for r0 in range(0, tt, rb):
            win = h_ref[pl.ds(r0, rb + halo), ls]
            acc = jnp.broadcast_to(b_ref[:, ls], (rb, LANES))
            for b in range(SUBLANES):
                shifted = win if b == 0 else pltpu.roll(win, b, axis=0)
                for a in range(halo // SUBLANES):
                    d = SUBLANES * a + b
                    if d > CONV_WIDTH - 1:
                        continue
                    j = CONV_WIDTH - 1 - d
                    lo = halo - SUBLANES * a
                    acc = acc + w_ref[j:j + 1, ls] * shifted[lo:lo + rb, :]
            c_ref[pl.ds(r0, rb), ls] = acc
        return carry

    lax.fori_loop(0, D_MODEL // LANES, lane_block, 0)
    h_ref[pl.ds(0, halo), :] = h_ref[pl.ds(tt, halo), :]
    y = _layer_norm(c_ref[...], lg_ref[...], lb_ref[...])
    o_ref[...] = (y * _sigmoid(y)).astype(o_ref.dtype)


def _conv(u2, w_dw, b_dw, ln_g, ln_b, layer, batch, seq):
    n = u2.shape[0]
    tt = CONV_TT
    nt = seq // tt
    vec = pl.BlockSpec((None, 1, D_MODEL), lambda b, t: (layer, 0, 0))
    return pl.pallas_call(
        _conv_kernel,
        name="conv",
        grid=(batch, nt),
        in_specs=[
            pl.BlockSpec((tt, D_MODEL), lambda b, t: (b * nt + t, 0)),
            pl.BlockSpec((tt, D_MODEL), lambda b, t: (b * nt + t, 1)),
            pl.BlockSpec((None, CONV_WIDTH, D_MODEL), lambda b, t: (layer, 0, 0)),
            vec, vec, vec,
        ],
        out_specs=pl.BlockSpec((tt, D_MODEL), lambda b, t: (b * nt + t, 0)),
        out_shape=jax.ShapeDtypeStruct((n, D_MODEL), BF16),
        scratch_shapes=[pltpu.VMEM((tt + CONV_HALO, D_MODEL), F32),
                        pltpu.VMEM((tt, D_MODEL), F32)],
        compiler_params=_cparams(("parallel", "arbitrary")),
    )(u2, u2, w_dw, b_dw, ln_g, ln_b)


MERGE_TM = 256


def _merge_kernel(a1_ref, a2_ref, ga_ref, gb_ref, x_ref, wo_ref, wpw_ref, wout_ref,
                  lg_ref, lb_ref, xo_ref):
    ya = _dot(a1_ref[...], wo_ref[...])
    yb = _dot(a2_ref[...], wpw_ref[...])
    z = _sigmoid(ga_ref[...].astype(F32)) * ya + _sigmoid(gb_ref[...].astype(F32)) * yb
    mix = _dot(z.astype(BF16), wout_ref[...])
    xo_ref[...] = _layer_norm(DN_ALPHA * x_ref[...] + mix, lg_ref[...], lb_ref[...])


def _merge(a1, a2, u2, x, w_o, w_pw, w_out, ln_g, ln_b, layer):
    n = x.shape[0]
    tm = MERGE_TM
    row = lambda c: pl.BlockSpec((tm, D_MODEL), lambda i: (i, c))
    wfull = pl.BlockSpec((None, D_MODEL, D_MODEL), lambda i: (layer, 0, 0),
                         pipeline_mode=pl.Buffered(1))
    vec = pl.BlockSpec((None, 1, D_MODEL), lambda i: (2 * layer, 0, 0))
    return pl.pallas_call(
        _merge_kernel,
        name="merge",
        grid=(n // tm,),
        in_specs=[row(0), row(0), row(2), row(3), row(0), wfull, wfull, wfull, vec, vec],
        out_specs=row(0),
        out_shape=jax.ShapeDtypeStruct((n, D_MODEL), F32),
        compiler_params=_cparams(("parallel",)),
    )(a1, a2, u2, u2, x, w_o, w_pw, w_out, ln_g, ln_b)


FFN_TM = 512
FFN_TF = 512
FFN_SPLIT = 2


def _swiglu_accumulate(xb_ref, wg_ref, wu_ref, wd_ref, acc_ref, tm):
    rows = tm // FFN_SPLIT
    wg, wu, wd = wg_ref[...], wu_ref[...], wd_ref[...]
    for s in range(FFN_SPLIT):
        sl = pl.ds(s * rows, rows)
        xb = xb_ref[sl, :]
        g = _dot(xb, wg)
        u = _dot(xb, wu)
        h = (g * _sigmoid(g) * u).astype(BF16)
        acc_ref[sl, :] += _dot(h, wd)


def _ffn_cast_kernel(wg_ref, wu_ref, wd_ref, x_ref, lg_ref, lb_ref, side_ref,
                     xo_ref, xbo_ref, side_out_ref, xb_ref, acc_ref):
    side_out_ref[...] = side_ref[...].astype(side_out_ref.dtype)
    _ffn_kernel(wg_ref, wu_ref, wd_ref, x_ref, lg_ref, lb_ref, xo_ref, xbo_ref, xb_ref, acc_ref)


def _ffn_kernel(wg_ref, wu_ref, wd_ref, x_ref, lg_ref, lb_ref, xo_ref, xbo_ref, xb_ref, acc_ref):
    f = pl.program_id(1)

    @pl.when(f == 0)
    def _():
        xb_ref[...] = x_ref[...].astype(BF16)
        acc_ref[...] = jnp.zeros_like(acc_ref)

    _swiglu_accumulate(xb_ref, wg_ref, wu_ref, wd_ref, acc_ref, FFN_TM)

    @pl.when(f == pl.num_programs(1) - 1)
    def _():
        y = _layer_norm(DN_ALPHA * x_ref[...] + acc_ref[...], lg_ref[...], lb_ref[...])
        xo_ref[...] = y
        xbo_ref[...] = y.astype(BF16)


def _ffn(x, wg, wu, wd, ln_g, ln_b, layer, side=None):
    n = x.shape[0]
    tm, tf = FFN_TM, FFN_TF
    il = layer // 2
    nf = FFN_DIM // tf
    row = pl.BlockSpec((tm, D_MODEL), lambda i, f: (i, 0))
    vec = pl.BlockSpec((None, 1, D_MODEL), lambda i, f: (2 * layer + 1, 0, 0))
    in_specs = [pl.BlockSpec((None, D_MODEL, tf), lambda i, f: (il, 0, f)),
                pl.BlockSpec((None, D_MODEL, tf), lambda i, f: (il, 0, f)),
                pl.BlockSpec((None, tf, D_MODEL), lambda i, f: (il, f, 0)),
                row, vec, vec]
    out_specs = [row, row]
    out_shape = [jax.ShapeDtypeStruct((n, D_MODEL), F32), jax.ShapeDtypeStruct((n, D_MODEL), BF16)]
    scratch = [pltpu.VMEM((tm, D_MODEL), BF16), pltpu.VMEM((tm, D_MODEL), F32)]
    if side is None:
        return pl.pallas_call(
            _ffn_kernel, name="ffn", grid=(n // tm, nf), in_specs=in_specs, out_specs=out_specs,
            out_shape=out_shape, scratch_shapes=scratch,
            compiler_params=_cparams(("parallel", "arbitrary")),
        )(wg, wu, wd, x, ln_g, ln_b)
    sspec = _side_spec(side, (n // tm) * nf, lambda i, f: i * nf + f)
    return pl.pallas_call(
        _ffn_cast_kernel, name="ffn_cast", grid=(n // tm, nf),
        in_specs=in_specs + [sspec], out_specs=out_specs + [sspec],
        out_shape=out_shape + [jax.ShapeDtypeStruct(side.shape, BF16)], scratch_shapes=scratch,
        compiler_params=_cparams(("arbitrary", "arbitrary")),
    )(wg, wu, wd, x, ln_g, ln_b, side)


ROUTER_TM = 1024


def _router_kernel(x_ref, w_ref, idx_ref, wt_ref):
    logits = _dot_3pass(x_ref[...], w_ref[...])
    lane = lax.broadcasted_iota(jnp.int32, logits.shape, 1)
    neg = jnp.float32(-jnp.inf)
    lg = jnp.where(lane < N_EXPERTS, logits, neg)
    m1 = jnp.max(lg, axis=-1, keepdims=True)
    i1 = jnp.min(jnp.where(lg == m1, lane, LANES), axis=-1, keepdims=True)
    lg2 = jnp.where(lane == i1, neg, lg)
    m2 = jnp.max(lg2, axis=-1, keepdims=True)
    i2 = jnp.min(jnp.where(lg2 == m2, lane, LANES), axis=-1, keepdims=True)
    e2 = jnp.exp(m2 - m1)
    w1 = 1.0 / (1.0 + e2)
    w2 = e2 / (1.0 + e2)
    idx_ref[...] = jnp.where(lane == 0, i1, jnp.where(lane == 1, i2, 0))
    wt_ref[...] = jnp.where(lane == 0, w1, jnp.where(lane == 1, w2, 0.0))


def _router(x, w_router_padded, layer):
    n = x.shape[0]
    tm = ROUTER_TM
    out = pl.BlockSpec((tm, LANES), lambda i: (i, 0))
    return pl.pallas_call(
        _router_kernel,
        name="router",
        grid=(n // tm,),
        in_specs=[pl.BlockSpec((tm, D_MODEL), lambda i: (i, 0)),
                  pl.BlockSpec((None, D_MODEL, LANES), lambda i: (layer // 2, 0, 0))],
        out_specs=[out, out],
        out_shape=[jax.ShapeDtypeStruct((n, LANES), jnp.int32),
                   jax.ShapeDtypeStruct((n, LANES), F32)],
        compiler_params=_cparams(("parallel",)),
    )(x, w_router_padded)


MOE_TM = 512
MOE_TF = 512
GATHER_UNROLL = 8


def _row_copy(src_hbm, dst_vmem, sem, src_row, dst_row):
    return pltpu.make_async_copy(src_hbm.at[pl.ds(src_row, 1), :],
                                 dst_vmem.at[pl.ds(dst_row, 1), :], sem)


def _gather_rows(src_hbm, idx_ref, dst_vmem, sem, n_rows):
    def issue(g, carry):
        base = pl.multiple_of(g * GATHER_UNROLL, GATHER_UNROLL)
        for k in range(GATHER_UNROLL):
            _row_copy(src_hbm, dst_vmem, sem, idx_ref[0, base + k], base + k).start(priority=k % 2)
        return carry
    lax.fori_loop(0, n_rows // GATHER_UNROLL, issue, 0)


def _wait_rows(src_hbm, dst_vmem, sem, n_rows):
    def drain(r, carry):
        _row_copy(src_hbm, dst_vmem, sem, 0, r).wait()
        return carry
    lax.fori_loop(0, n_rows, drain, 0, unroll=GATHER_UNROLL)


def _moe_kernel(be_ref, nu_ref, tok_ref, tokn_ref, x_hbm, wg_ref, wu_ref, wd_ref, y_ref,
                xf_ref, xb_ref, sem):
    i = pl.program_id(0)
    f = pl.program_id(1)
    last = pl.num_programs(1) - 1
    n_used = nu_ref[0]
    used = i < n_used
    slot = lax.rem(i, 2)

    @pl.when(used & (f == 0))
    def _():
        @pl.when(i == 0)
        def _():
            _gather_rows(x_hbm, tok_ref, xf_ref.at[0], sem.at[0], MOE_TM)

        _wait_rows(x_hbm, xf_ref.at[slot], sem.at[slot], MOE_TM)
        xb_ref[...] = xf_ref[slot].astype(BF16)
        y_ref[...] = jnp.zeros_like(y_ref)

        @pl.when(i + 1 < n_used)
        def _():
            _gather_rows(x_hbm, tokn_ref, xf_ref.at[1 - slot], sem.at[1 - slot], MOE_TM)

    @pl.when(used)
    def _():
        _swiglu_accumulate(xb_ref, wg_ref, wu_ref, wd_ref, y_ref, MOE_TM)

    @pl.when(jnp.logical_not(used) & (f == last))
    def _():
        y_ref[...] = jnp.zeros_like(y_ref)


def _moe_ffn(block_e, n_used, row_tok, x, wg, wu, wd, layer):
    n_rows = row_tok.shape[0]
    tm, tf = MOE_TM, MOE_TF
    nb = n_rows // tm
    nf = FFN_DIM // tf
    il = layer // 2
    tok3 = row_tok.reshape(nb, 1, tm)

    def fsel(i, f, nu):
        return jnp.where(i < nu[0], f, nf - 1)

    grid_spec = pltpu.PrefetchScalarGridSpec(
        num_scalar_prefetch=2,
        grid=(nb, nf),
        in_specs=[
            pl.BlockSpec((None, 1, tm), lambda i, f, be, nu: (i, 0, 0), memory_space=pltpu.SMEM),
            pl.BlockSpec((None, 1, tm), lambda i, f, be, nu: (jnp.minimum(i + 1, nb - 1), 0, 0),
                         memory_space=pltpu.SMEM),
            pl.BlockSpec(memory_space=pl.ANY),
            pl.BlockSpec((None, None, D_MODEL, tf),
                         lambda i, f, be, nu: (il, be[i], 0, fsel(i, f, nu))),
            pl.BlockSpec((None, None, D_MODEL, tf),
                         lambda i, f, be, nu: (il, be[i], 0, fsel(i, f, nu))),
            pl.BlockSpec((None, None, tf, D_MODEL),
                         lambda i, f, be, nu: (il, be[i], fsel(i, f, nu), 0)),
        ],
        out_specs=pl.BlockSpec((tm, D_MODEL), lambda i, f, be, nu: (i, 0)),
        scratch_shapes=[pltpu.VMEM((2, tm, D_MODEL), F32),
                        pltpu.VMEM((tm, D_MODEL), BF16),
                        pltpu.SemaphoreType.DMA((2,))],
    )
    return pl.pallas_call(
        _moe_kernel,
        name="moe_ffn",
        grid_spec=grid_spec,
        out_shape=jax.ShapeDtypeStruct((n_rows, D_MODEL), F32),
        compiler_params=_cparams(("arbitrary", "arbitrary")),
    )(block_e, n_used, tok3, tok3, x, wg, wu, wd)


COMB_TT = 256


def _combine_kernel(pos0_ref, pos1_ref, pos0n_ref, pos1n_ref, y_hbm, wt_ref, x_ref, lg_ref, lb_ref,
                    xo_ref, xbo_ref, g_ref, sem):
    tt = COMB_TT
    i = pl.program_id(0)
    slot = lax.rem(i, 2)

    @pl.when(i == 0)
    def _():
        _gather_rows(y_hbm, pos0_ref, g_ref.at[0, 0], sem.at[0, 0], tt)
        _gather_rows(y_hbm, pos1_ref, g_ref.at[0, 1], sem.at[0, 1], tt)

    @pl.when(i + 1 < pl.num_programs(0))
    def _():
        _gather_rows(y_hbm, pos0n_ref, g_ref.at[1 - slot, 0], sem.at[1 - slot, 0], tt)
        _gather_rows(y_hbm, pos1n_ref, g_ref.at[1 - slot, 1], sem.at[1 - slot, 1], tt)

    _wait_rows(y_hbm, g_ref.at[slot, 0], sem.at[slot, 0], tt)
    _wait_rows(y_hbm, g_ref.at[slot, 1], sem.at[slot, 1], tt)
    wt = wt_ref[...]
    f = wt[:, 0:1] * g_ref[slot, 0] + wt[:, 1:2] * g_ref[slot, 1]
    y = _layer_norm(DN_ALPHA * x_ref[...] + f, lg_ref[...], lb_ref[...])
    xo_ref[...] = y
    xbo_ref[...] = y.astype(BF16)


def _combine(pos, yrows, wt, x, ln_g, ln_b, layer):
    n = x.shape[0]
    tt = COMB_TT
    pos0 = pos[0::TOP_K].reshape(n // tt, 1, tt)
    pos1 = pos[1::TOP_K].reshape(n // tt, 1, tt)
    row = pl.BlockSpec((tt, D_MODEL), lambda i: (i, 0))
    vec = pl.BlockSpec((None, 1, D_MODEL), lambda i: (2 * layer + 1, 0, 0))
    nt = n // tt
    idx = pl.BlockSpec((None, 1, tt), lambda i: (i, 0, 0), memory_space=pltpu.SMEM)
    idx_next = pl.BlockSpec((None, 1, tt), lambda i: (jnp.minimum(i + 1, nt - 1), 0, 0),
                            memory_space=pltpu.SMEM)
    return pl.pallas_call(
        _combine_kernel,
        name="combine",
        grid=(nt,),
        in_specs=[idx, idx, idx_next, idx_next,
                  pl.BlockSpec(memory_space=pl.ANY),
                  pl.BlockSpec((tt, LANES), lambda i: (i, 0)),
                  row, vec, vec],
        out_specs=[row, row],
        out_shape=[jax.ShapeDtypeStruct((n, D_MODEL), F32),
                   jax.ShapeDtypeStruct((n, D_MODEL), BF16)],
        scratch_shapes=[pltpu.VMEM((2, TOP_K, tt, D_MODEL), F32),
                        pltpu.SemaphoreType.DMA((2, TOP_K))],
        compiler_params=_cparams(("arbitrary",)),
    )(pos0, pos1, pos0, pos1, yrows, wt, x, ln_g, ln_b)


def _routing_plan(idx, n):
    n_assign = n * TOP_K
    e_flat = idx[:, :TOP_K].reshape(-1)
    onehot = (e_flat[:, None] == jnp.arange(N_EXPERTS, dtype=jnp.int32)[None, :]).astype(jnp.int32)
    csum = jnp.cumsum(onehot, axis=0)
    rank = jnp.sum(csum * onehot, axis=1) - 1
    counts = csum[-1]
    padded = (counts + MOE_TM - 1) // MOE_TM * MOE_TM
    pend = jnp.cumsum(padded)
    pstart = pend - padded
    pos = (jnp.sum(pstart[None, :] * onehot, axis=1) + rank).astype(jnp.int32)
    n_rows = (-(-n_assign // MOE_TM) + N_EXPERTS) * MOE_TM
    nb = n_rows // MOE_TM
    row_tok = jnp.zeros((n_rows,), jnp.int32).at[pos].set(
        jnp.arange(n_assign, dtype=jnp.int32) // TOP_K, unique_indices=True,
        mode="promise_in_bounds")
    block_start = jnp.arange(nb, dtype=jnp.int32) * MOE_TM
    block_e = jnp.minimum(jnp.sum((pend[None, :] <= block_start[:, None]).astype(jnp.int32), axis=1),
                          N_EXPERTS - 1).astype(jnp.int32)
    n_used = (pend[-1:] // MOE_TM).astype(jnp.int32)
    return pos, row_tok, block_e, n_used


def kernel(x, w_in, gla_a_up, gla_a_bias, gla_norm_g, gla_w_o, conv_w_dw, conv_b_dw, conv_ln_g, conv_ln_b, conv_w_pw, w_out, ln_g, ln_b, ffn_w_gate, ffn_w_up, ffn_w_down, moe_w_router, moe_w_gate, moe_w_up, moe_w_down):
    batch, seq, d = x.shape
    n = batch * seq
    xf = x.reshape(n, d)
    xb = xf.astype(BF16)
    a_lo, a_hi = QKVR, QKVR + GLA_RANK
    w_in_b = w_in.astype(BF16)
    w_rest = w_in_b[:, :, a_hi:]
    w_alow = jnp.pad(w_in_b[:, :, a_lo:a_hi], ((0, 0), (0, 0), (0, LANES - GLA_RANK)))
    wup = jnp.pad(gla_a_up, ((0, 0), (0, LANES - GLA_RANK), (0, 0)))
    w_o, w_pw, w_mix = gla_w_o.astype(BF16), conv_w_pw.astype(BF16), w_out.astype(BF16)
    ffn_g, ffn_u, ffn_d = ffn_w_gate.astype(BF16), ffn_w_up.astype(BF16), ffn_w_down.astype(BF16)
    moe_side = {(0, "mm"): moe_w_gate.reshape(-1, FFN_DIM), (1, "mm"): moe_w_up.reshape(-1, FFN_DIM),
                (0, "ffn"): moe_w_down.reshape(-1, D_MODEL)}
    moe_b = {}
    w_r = jnp.pad(moe_w_router, ((0, 0), (0, 0), (0, LANES - N_EXPERTS)))
    ba, gn = gla_a_bias[:, None, :], gla_norm_g[:, None, :]
    cb, cg, cbeta = conv_b_dw[:, None, :], conv_ln_g[:, None, :], conv_ln_b[:, None, :]
    lng = ln_g.reshape(2 * DEPTH, 1, d)
    lnb = ln_b.reshape(2 * DEPTH, 1, d)
    for layer in range(DEPTH):
        u1 = _matmul(xb, w_in_b, layer, BF16, 1024, 1024, n=QKVR)
        if (layer, "mm") in moe_side:
            u2, moe_b[layer, "mm"] = _matmul(xb, w_rest, layer, BF16, 1024, 1024,
                                             side=moe_side[layer, "mm"])
        else:
            u2 = _matmul(xb, w_rest, layer, BF16, 1024, 1024)
        a_low = _matmul(xb, w_alow, layer, F32, 1024, LANES)
        cum = _decay(a_low, wup, ba, layer)
        a1 = _gla(u1, cum, gn, layer, batch, seq)
        a2 = _conv(u2, conv_w_dw, cb, cg, cbeta, layer, batch, seq)
        xf = _merge(a1, a2, u2, xf, w_o, w_pw, w_mix, lng, lnb, layer)
        if layer % 2 == 0:
            if (layer, "ffn") in moe_side:
                xf, xb, moe_b[layer, "ffn"] = _ffn(xf, ffn_g, ffn_u, ffn_d, lng, lnb, layer,
                                                   side=moe_side[layer, "ffn"])
            else:
                xf, xb = _ffn(xf, ffn_g, ffn_u, ffn_d, lng, lnb, layer)
        else:
            moe_g = moe_b[0, "mm"].reshape(moe_w_gate.shape)
            moe_u = moe_b[1, "mm"].reshape(moe_w_up.shape)
            moe_d = moe_b[0, "ffn"].reshape(moe_w_down.shape)
            idx, wt = _router(xf, w_r, layer)
            pos, row_tok, block_e, n_used = _routing_plan(idx, n)
            yrows = _moe_ffn(block_e, n_used, row_tok, xf, moe_g, moe_u, moe_d, layer)
            xf, xb = _combine(pos, yrows, wt, xf, lng, lnb, layer)
    return xf.reshape(batch, seq, d)
```
